```python
import jax, jax.numpy as jnp
from jax import lax
import numpy as np

D_MODEL = 1024
BATCH = 8
SEQ = 4096
DEPTH = 1
DEC_BATCH = 2
DEC_SEQ = 8192
PAST_LEN = 128

GRID_W = 64
HEAD_DIM = 64
N_Q_HEADS = 8
N_KV_HEADS = 2
Q_PER_KV = N_Q_HEADS // N_KV_HEADS
ATTN_WIDTH = N_Q_HEADS * HEAD_DIM
KV_WIDTH = N_KV_HEADS * HEAD_DIM
POOL_WINDOWS = (2, 4, 8, 16)
N_POOL_GROUPS = len(POOL_WINDOWS)
POOL_WIDTH = D_MODEL - ATTN_WIDTH
POOL_GROUP_DIM = POOL_WIDTH // N_POOL_GROUPS
MIX_WIDTH = ATTN_WIDTH + POOL_WIDTH
IN_WIDTH = ATTN_WIDTH + 2 * KV_WIDTH + POOL_WIDTH
D_FF = 4 * D_MODEL
Q_BLOCK = 128
ROPE_THETA = 10000.0
EPS = 1e-6

kernel_name = "hymba_attn_pool_encoder"


def rmsnorm(x, g):
    xf = x.astype(jnp.float32)
    y = xf * lax.rsqrt(jnp.mean(xf * xf, axis=-1, keepdims=True) + EPS) * g.astype(jnp.float32)
    return y.astype(x.dtype)


def axial_rope_tables(seq_len):
    rows = seq_len // GRID_W
    row_ids = jnp.broadcast_to(jnp.arange(rows, dtype=jnp.float32)[:, None], (rows, GRID_W)).reshape(-1)
    col_ids = jnp.broadcast_to(jnp.arange(GRID_W, dtype=jnp.float32)[None, :], (rows, GRID_W)).reshape(-1)
    n_freq = HEAD_DIM // 4
    inv_freq = ROPE_THETA ** (-jnp.arange(n_freq, dtype=jnp.float32) / n_freq)
    ang = jnp.stack([row_ids[:, None] * inv_freq, col_ids[:, None] * inv_freq], axis=1)
    return jnp.cos(ang), jnp.sin(ang)


def apply_axial_rope(x, cos, sin):
    B, S, H, _ = x.shape
    xr = x.reshape(B, S, H, 2, 2, HEAD_DIM // 4)
    x1, x2 = xr[..., 0, :], xr[..., 1, :]
    c = cos[None, :, None].astype(x.dtype)
    s = sin[None, :, None].astype(x.dtype)
    out = jnp.stack([x1 * c - x2 * s, x2 * c + x1 * s], axis=-2)
    return out.reshape(B, S, H, HEAD_DIM)


def block_swept_gqa(q, k, v):
    B, S = q.shape[:2]
    nblk = S // Q_BLOCK
    scale = HEAD_DIM ** -0.5
    qb = q.reshape(B, nblk, Q_BLOCK, N_KV_HEADS, Q_PER_KV, HEAD_DIM).transpose(1, 0, 2, 3, 4, 5)

    def one_block(qi):
        s = jnp.einsum('bqkgd,bskd->bkgqs', qi, k, preferred_element_type=jnp.float32) * scale
        p = jax.nn.softmax(s, axis=-1).astype(v.dtype)
        return jnp.einsum('bkgqs,bskd->bqkgd', p, v)

    ob = lax.map(one_block, qb)
    return ob.transpose(1, 0, 2, 3, 4, 5).reshape(B, S, ATTN_WIDTH)


def multiscale_pool(u, w_pool, pool_scale):
    B, S, _ = u.shape
    uf = u.astype(jnp.float32).reshape(B, S, N_POOL_GROUPS, POOL_GROUP_DIM)
    cs = jnp.concatenate([jnp.zeros((B, 1, N_POOL_GROUPS, POOL_GROUP_DIM), jnp.float32),
                          jnp.cumsum(uf, axis=1)], axis=1)
    t = jnp.arange(S)[:, None]
    win = jnp.array(POOL_WINDOWS, dtype=jnp.int32)[None, :]
    lo = jnp.clip(t - win // 2, 0, S)
    hi = jnp.clip(t - win // 2 + win, 0, S)
    cnt = (hi - lo).astype(jnp.float32)
    g_idx = jnp.arange(N_POOL_GROUPS)[None, :]
    wsum = cs[:, hi, g_idx, :] - cs[:, lo, g_idx, :]
    d = wsum / cnt[None, :, :, None] - uf
    z = jnp.einsum('bsgc,gce->bsge', d, w_pool.astype(jnp.float32))
    z = z.reshape(B, S, POOL_WIDTH) * pool_scale.astype(jnp.float32)
    return z.astype(u.dtype)


def encoder_layer(x, norm1_g, w_in, q_norm_g, k_norm_g, w_pool, pool_scale, w_out,
                  norm2_g, w_up, w_down, cos, sin):
    B, S, _ = x.shape
    h = rmsnorm(x, norm1_g)
    proj = h @ w_in
    q = proj[..., :ATTN_WIDTH].reshape(B, S, N_Q_HEADS, HEAD_DIM)
    k = proj[..., ATTN_WIDTH:ATTN_WIDTH + KV_WIDTH].reshape(B, S, N_KV_HEADS, HEAD_DIM)
    v = proj[..., ATTN_WIDTH + KV_WIDTH:ATTN_WIDTH + 2 * KV_WIDTH].reshape(B, S, N_KV_HEADS, HEAD_DIM)
    u = proj[..., ATTN_WIDTH + 2 * KV_WIDTH:]
    q = apply_axial_rope(rmsnorm(q, q_norm_g), cos, sin)
    k = apply_axial_rope(rmsnorm(k, k_norm_g), cos, sin)
    a = block_swept_gqa(q, k, v)
    p = multiscale_pool(u, w_pool, pool_scale)
    x = x + jnp.concatenate([a, p], axis=-1) @ w_out
    h2 = rmsnorm(x, norm2_g)
    x = x + jnp.square(jax.nn.relu(h2 @ w_up)) @ w_down
    return x


def run_trunk(x, norm1_g, w_in, q_norm_g, k_norm_g, w_pool, pool_scale, w_out,
              norm2_g, w_up, w_down):
    cos, sin = axial_rope_tables(x.shape[1])
    for l in range(DEPTH):
        x = encoder_layer(x, norm1_g[l], w_in[l], q_norm_g[l], k_norm_g[l], w_pool[l],
                          pool_scale[l], w_out[l], norm2_g[l], w_up[l], w_down[l], cos, sin)
    return x


def setup_inputs(seed: int = 0) -> dict:
    key = jax.random.key(seed)
    ks = jax.random.split(key, 13)
    f32 = jnp.float32

    def nrm(k, shape, scale):
        return jax.random.normal(k, shape, f32) * scale

    return {
        "x_prompt": nrm(ks[0], (BATCH, SEQ, D_MODEL), 1.0),
        "x_sample": nrm(ks[1], (DEC_BATCH, DEC_SEQ, D_MODEL), 1.0),
        "norm1_g": 1.0 + nrm(ks[2], (DEPTH, D_MODEL), 0.05),
        "w_in": nrm(ks[3], (DEPTH, D_MODEL, IN_WIDTH), D_MODEL ** -0.5),
        "q_norm_g": 1.0 + nrm(ks[4], (DEPTH, HEAD_DIM), 0.05),
        "k_norm_g": 1.0 + nrm(ks[5], (DEPTH, HEAD_DIM), 0.05),
        "w_pool": nrm(ks[6], (DEPTH, N_POOL_GROUPS, POOL_GROUP_DIM, POOL_GROUP_DIM), POOL_GROUP_DIM ** -0.5),
        "pool_scale": 1.0 + nrm(ks[7], (DEPTH, POOL_WIDTH), 0.1),
        "w_out": nrm(ks[8], (DEPTH, MIX_WIDTH, D_MODEL), MIX_WIDTH ** -0.5),
        "norm2_g": 1.0 + nrm(ks[9], (DEPTH, D_MODEL), 0.05),
        "w_up": nrm(ks[10], (DEPTH, D_MODEL, D_FF), D_MODEL ** -0.5),
        "w_down": nrm(ks[11], (DEPTH, D_FF, D_MODEL), D_FF ** -0.5),
    }


def reference(x_prompt, x_sample, norm1_g, w_in, q_norm_g, k_norm_g, w_pool, pool_scale,
              w_out, norm2_g, w_up, w_down):
    y_prompt = run_trunk(x_prompt, norm1_g, w_in, q_norm_g, k_norm_g, w_pool, pool_scale,
                         w_out, norm2_g, w_up, w_down)
    y_sample = run_trunk(x_sample, norm1_g, w_in, q_norm_g, k_norm_g, w_pool, pool_scale,
                         w_out, norm2_g, w_up, w_down)
    return (y_prompt, y_sample)
```

```python
import functools
import math

import jax
import jax.numpy as jnp
from jax.experimental import pallas as pl
from jax.experimental.pallas import tpu as pltpu

D_MODEL = 1024
GRID_W = 64
HEAD_DIM = 64
N_Q_HEADS = 8
N_KV_HEADS = 2
Q_PER_KV = N_Q_HEADS // N_KV_HEADS
ATTN_WIDTH = N_Q_HEADS * HEAD_DIM
KV_WIDTH = N_KV_HEADS * HEAD_DIM
POOL_WINDOWS = (2, 4, 8, 16)
N_POOL_GROUPS = len(POOL_WINDOWS)
POOL_WIDTH = D_MODEL - ATTN_WIDTH
POOL_GROUP_DIM = POOL_WIDTH // N_POOL_GROUPS
IN_WIDTH = ATTN_WIDTH + 2 * KV_WIDTH + POOL_WIDTH
D_FF = 4 * D_MODEL
ROPE_THETA = 10000.0
EPS = 1e-6

LANES = 128
SUBLANES = 8
POOL_HALO = 8
VMEM_LIMIT_BYTES = 56 * 1024 * 1024

SEQ_TILE = 512
KEY_TILE = 512
Q_TILE = 128
FF_CHUNK = 1024

F32 = jnp.float32
BF16 = jnp.bfloat16


def _rope_tables(seq_len):
    rows = seq_len // GRID_W
    row_ids = jnp.broadcast_to(jnp.arange(rows, dtype=F32)[:, None], (rows, GRID_W)).reshape(-1)
    col_ids = jnp.broadcast_to(jnp.arange(GRID_W, dtype=F32)[None, :], (rows, GRID_W)).reshape(-1)
    n_freq = HEAD_DIM // 4
    inv_freq = ROPE_THETA ** (-jnp.arange(n_freq, dtype=F32) / n_freq)
    ang = jnp.stack([row_ids[:, None] * inv_freq, col_ids[:, None] * inv_freq], axis=1)
    cos, sin = jnp.cos(ang), jnp.sin(ang)
    zero = jnp.zeros_like(sin)
    cos_h = jnp.stack([cos, cos], axis=2).reshape(seq_len, HEAD_DIM)
    sin_up = jnp.stack([-sin, zero], axis=2).reshape(seq_len, HEAD_DIM)
    sin_dn = jnp.stack([zero, sin], axis=2).reshape(seq_len, HEAD_DIM)
    rep = LANES // HEAD_DIM
    return (jnp.tile(cos_h, (1, rep)), jnp.tile(sin_up, (1, rep)), jnp.tile(sin_dn, (1, rep)))


def _head_mean_matrix():
    idx = jnp.arange(LANES) // HEAD_DIM
    return jnp.where(idx[:, None] == idx[None, :], 1.0 / HEAD_DIM, 0.0).astype(BF16)


def _in_proj_kernel(x_ref, g1_ref, w_in_ref, qg_ref, kg_ref, hm_ref, cos_ref, sup_ref, sdn_ref,
                    q_ref, kt_ref, v_ref, u_ref):
    x = x_ref[0]
    ms = jnp.mean(x * x, axis=-1, keepdims=True)
    h = (x * jax.lax.rsqrt(ms + EPS) * g1_ref[...]).astype(BF16)
    proj = jnp.dot(h, w_in_ref[...], preferred_element_type=F32)

    hm = hm_ref[...]
    cos, sup, sdn = cos_ref[...], sup_ref[...], sdn_ref[...]
    n_freq = HEAD_DIM // 4

    def norm_rope(t, gain):
        t2 = t * t
        hi = t2.astype(BF16)
        lo = (t2 - hi.astype(F32)).astype(BF16)
        ms_h = (jnp.dot(hi, hm, preferred_element_type=F32)
                + jnp.dot(lo, hm, preferred_element_type=F32))
        tn = t * jax.lax.rsqrt(ms_h + EPS) * gain
        return (tn * cos + pltpu.roll(tn, LANES - n_freq, 1) * sup
                + pltpu.roll(tn, n_freq, 1) * sdn)

    q_scale = HEAD_DIM ** -0.5 * math.log2(math.e)
    for c in range(ATTN_WIDTH // LANES):
        sl = slice(c * LANES, (c + 1) * LANES)
        qc = norm_rope(proj[:, sl], qg_ref[:, sl]) * q_scale
        q_ref[0, :, sl] = qc.astype(BF16)

    k = norm_rope(proj[:, ATTN_WIDTH:ATTN_WIDTH + KV_WIDTH], kg_ref[...])
    kt = k.T
    for g in range(N_KV_HEADS):
        kt_ref[0, g, 0] = kt[g * HEAD_DIM:(g + 1) * HEAD_DIM, :].astype(BF16)

    v = proj[:, ATTN_WIDTH + KV_WIDTH:ATTN_WIDTH + 2 * KV_WIDTH]
    lane = jax.lax.broadcasted_iota(jnp.int32, v.shape, 1)
    for g in range(N_KV_HEADS):
        vg = v if g == 0 else pltpu.roll(v, LANES - g * HEAD_DIM, 1)
        vg = jnp.where(lane < HEAD_DIM, vg, jnp.where(lane == HEAD_DIM, 1.0, 0.0))
        v_ref[0, g] = vg.astype(BF16)

    u_ref[0] = proj[:, ATTN_WIDTH + 2 * KV_WIDTH:]


def _in_proj(x, g1, w_in, qg, kg, hm, cos, sup, sdn):
    B, S, _ = x.shape
    T = SEQ_TILE
    assert S % T == 0 and T == KEY_TILE
    nt = S // T
    const = lambda b, i: (0, 0)
    return pl.pallas_call(
        _in_proj_kernel,
        grid=(B, nt),
        in_specs=[
            pl.BlockSpec((1, T, D_MODEL), lambda b, i: (b, i, 0)),
            pl.BlockSpec((1, D_MODEL), const),
            pl.BlockSpec((D_MODEL, IN_WIDTH), const),
            pl.BlockSpec((1, ATTN_WIDTH), const),
            pl.BlockSpec((1, KV_WIDTH), const),
            pl.BlockSpec((LANES, LANES), const),
            pl.BlockSpec((T, LANES), lambda b, i: (i, 0)),
            pl.BlockSpec((T, LANES), lambda b, i: (i, 0)),
            pl.BlockSpec((T, LANES), lambda b, i: (i, 0)),
        ],
        out_specs=[
            pl.BlockSpec((1, T, ATTN_WIDTH), lambda b, i: (b, i, 0)),
            pl.BlockSpec((1, N_KV_HEADS, 1, HEAD_DIM, T), lambda b, i: (b, 0, i, 0, 0)),
            pl.BlockSpec((1, N_KV_HEADS, T, LANES), lambda b, i: (b, 0, i, 0)),
            pl.BlockSpec((1, T, POOL_WIDTH), lambda b, i: (b, i, 0)),
        ],
        out_shape=[
            jax.ShapeDtypeStruct((B, S, ATTN_WIDTH), BF16),
            jax.ShapeDtypeStruct((B, N_KV_HEADS, nt, HEAD_DIM, T), BF16),
            jax.ShapeDtypeStruct((B, N_KV_HEADS, S, LANES), BF16),
            jax.ShapeDtypeStruct((B, S, POOL_WIDTH), F32),
        ],
        compiler_params=pltpu.CompilerParams(
            dimension_semantics=("parallel", "parallel"),
            vmem_limit_bytes=VMEM_LIMIT_BYTES),
        name="in_proj",
    )(x, g1, w_in, qg, kg, hm, cos, sup, sdn)


def _attn_kernel(q_ref, kt_ref, v_ref, o_ref):
    n_kt = kt_ref.shape[2]
    tq = q_ref.shape[1]
    q = q_ref[0]
    qs = jnp.concatenate([q[:, h * HEAD_DIM:(h + 1) * HEAD_DIM] for h in range(Q_PER_KV)], axis=0)
    rows = Q_PER_KV * tq

    def step(j, carry):
        m, acc = carry
        s = jnp.dot(qs, kt_ref[0, 0, j], preferred_element_type=F32)
        m_new = jnp.maximum(m, jnp.max(s, axis=-1, keepdims=True))
        alpha = jnp.exp2(m - m_new)
        p = jnp.exp2(s - m_new).astype(BF16)
        start = pl.multiple_of(j * KEY_TILE, KEY_TILE)
        pv = jnp.dot(p, v_ref[0, 0, pl.ds(start, KEY_TILE), :], preferred_element_type=F32)
        return m_new, alpha * acc + pv

    m0 = jnp.full((rows, 1), -1e30, F32)
    acc0 = jnp.zeros((rows, LANES), F32)
    _, acc = jax.lax.fori_loop(0, n_kt, step, (m0, acc0))
    out = acc[:, :HEAD_DIM] / acc[:, HEAD_DIM:HEAD_DIM + 1]
    o_ref[0] = jnp.concatenate([out[h * tq:(h + 1) * tq] for h in range(Q_PER_KV)],
                               axis=1).astype(o_ref.dtype)


def _attention(q, kt, v):
    B, S, _ = q.shape
    n_kt = kt.shape[2]
    tq = Q_TILE
    gw = Q_PER_KV * HEAD_DIM
    return pl.pallas_call(
        _attn_kernel,
        grid=(B, N_KV_HEADS, S // tq),
        in_specs=[
            pl.BlockSpec((1, tq, gw), lambda b, g, i: (b, i, g)),
            pl.BlockSpec((1, 1, n_kt, HEAD_DIM, KEY_TILE), lambda b, g, i: (b, g, 0, 0, 0)),
            pl.BlockSpec((1, 1, S, LANES), lambda b, g, i: (b, g, 0, 0)),
        ],
        out_specs=pl.BlockSpec((1, tq, gw), lambda b, g, i: (b, i, g)),
        out_shape=jax.ShapeDtypeStruct((B, S, ATTN_WIDTH), BF16),
        compiler_params=pltpu.CompilerParams(
            dimension_semantics=("parallel", "parallel", "arbitrary"),
            vmem_limit_bytes=VMEM_LIMIT_BYTES),
        name="attention",
    )(q, kt, v)


def _mix_mlp_kernel(x_ref, a_ref, u_ref, up_ref, un_ref, wp_ref, ps_ref, wo_ref, g2_ref,
                    wu_ref, wd_ref, y_ref, ue_ref):
    i = pl.program_id(1)
    nt = pl.num_programs(1)
    T = x_ref.shape[1]
    S = T * nt

    u = u_ref[0]
    ue_ref[POOL_HALO:POOL_HALO + T, :] = u
    ue_ref[0:POOL_HALO, :] = jnp.where(i > 0, up_ref[0], 0.0)
    ue_ref[POOL_HALO + T:, :] = jnp.where(i < nt - 1, un_ref[0], 0.0)

    t = i * T + jax.lax.broadcasted_iota(jnp.int32, (T, 1), 0)
    mix = jnp.dot(a_ref[0], wo_ref[0:ATTN_WIDTH, :], preferred_element_type=F32)
    for g, win in enumerate(POOL_WINDOWS):
        sl = slice(g * POOL_GROUP_DIM, (g + 1) * POOL_GROUP_DIM)
        half = win // 2
        wsum = ue_ref[POOL_HALO - half:POOL_HALO - half + T, sl]
        for j in range(1 - half, half):
            wsum = wsum + ue_ref[POOL_HALO + j:POOL_HALO + j + T, sl]
        cnt = (jnp.minimum(t + half, S) - jnp.maximum(t - half, 0)).astype(F32)
        d = wsum / cnt - u[:, sl]
        z = jnp.dot(d.astype(BF16), wp_ref[g], preferred_element_type=F32) * ps_ref[:, sl]
        mix = mix + jnp.dot(z.astype(BF16), wo_ref[ATTN_WIDTH + g * POOL_GROUP_DIM:
                                                   ATTN_WIDTH + (g + 1) * POOL_GROUP_DIM, :],
                            preferred_element_type=F32)

    x1 = x_ref[0] + mix
    ms = jnp.mean(x1 * x1, axis=-1, keepdims=True)
    h2 = (x1 * jax.lax.rsqrt(ms + EPS) * g2_ref[...]).astype(BF16)
    acc = x1
    for c in range(D_FF // FF_CHUNK):
        sl = slice(c * FF_CHUNK, (c + 1) * FF_CHUNK)
        f = jnp.dot(h2, wu_ref[:, sl], preferred_element_type=F32)
        f = jnp.square(jnp.maximum(f, 0.0)).astype(BF16)
        acc = acc + jnp.dot(f, wd_ref[sl, :], preferred_element_type=F32)
    y_ref[0] = acc


def _mix_mlp(x, a, u, w_pool, pool_scale, w_out, g2, w_up, w_down):
    B, S, _ = x.shape
    T = SEQ_TILE
    nt = S // T
    hb = T // POOL_HALO
    n_hb = S // POOL_HALO
    const2 = lambda b, i: (0, 0)
    resident = functools.partial(pl.BlockSpec, pipeline_mode=pl.Buffered(1))
    return pl.pallas_call(
        _mix_mlp_kernel,
        grid=(B, nt),
        in_specs=[
            pl.BlockSpec((1, T, D_MODEL), lambda b, i: (b, i, 0)),
            pl.BlockSpec((1, T, ATTN_WIDTH), lambda b, i: (b, i, 0)),
            pl.BlockSpec((1, T, POOL_WIDTH), lambda b, i: (b, i, 0)),
            pl.BlockSpec((1, POOL_HALO, POOL_WIDTH),
                         lambda b, i: (b, jnp.maximum(i * hb - 1, 0), 0)),
            pl.BlockSpec((1, POOL_HALO, POOL_WIDTH),
                         lambda b, i: (b, jnp.minimum((i + 1) * hb, n_hb - 1), 0)),
            resident((N_POOL_GROUPS, POOL_GROUP_DIM, POOL_GROUP_DIM), lambda b, i: (0, 0, 0)),
            pl.BlockSpec((1, POOL_WIDTH), const2),
            resident((D_MODEL, D_MODEL), const2),
            pl.BlockSpec((1, D_MODEL), const2),
            resident((D_MODEL, D_FF), const2),
            resident((D_FF, D_MODEL), const2),
        ],
        out_specs=pl.BlockSpec((1, T, D_MODEL), lambda b, i: (b, i, 0)),
        out_shape=jax.ShapeDtypeStruct((B, S, D_MODEL), F32),
        scratch_shapes=[pltpu.VMEM((T + 2 * POOL_HALO, POOL_WIDTH), F32)],
        compiler_params=pltpu.CompilerParams(
            dimension_semantics=("parallel", "parallel"),
            vmem_limit_bytes=VMEM_LIMIT_BYTES),
        name="mix_mlp",
    )(x, a, u, u, u, w_pool, pool_scale, w_out, g2, w_up, w_down)


def _layer(x, g1, w_in, qg, kg, w_pool, pool_scale, w_out, g2, w_up, w_down, hm, tables):
    q, kt, v, u = _in_proj(x, g1, w_in, qg, kg, hm, *tables)
    a = _attention(q, kt, v)
    return _mix_mlp(x, a, u, w_pool, pool_scale, w_out, g2, w_up, w_down)


def _trunk(x, layers, hm):
    tables = _rope_tables(x.shape[1])
    for params in layers:
        x = _layer(x, *params, hm, tables)
    return x


def kernel(x_prompt, x_sample, norm1_g, w_in, q_norm_g, k_norm_g, w_pool, pool_scale, w_out,
           norm2_g, w_up, w_down):
    depth = w_in.shape[0]
    layers = []
    for l in range(depth):
        layers.append((
            norm1_g[l][None, :],
            w_in[l].astype(BF16),
            jnp.tile(q_norm_g[l], N_Q_HEADS)[None, :],
            jnp.tile(k_norm_g[l], N_KV_HEADS)[None, :],
            w_pool[l].astype(BF16),
            pool_scale[l][None, :],
            w_out[l].astype(BF16),
            norm2_g[l][None, :],
            w_up[l].astype(BF16),
            w_down[l].astype(BF16),
        ))
    hm = _head_mean_matrix()
    return (_trunk(x_prompt, layers, hm), _trunk(x_sample, layers, hm))
```

```python
import functools
import math

import jax
import jax.numpy as jnp
from jax.experimental import pallas as pl
from jax.experimental.pallas import tpu as pltpu

D_MODEL = 1024
GRID_W = 64
HEAD_DIM = 64
N_Q_HEADS = 8
N_KV_HEADS = 2
Q_PER_KV = N_Q_HEADS // N_KV_HEADS
ATTN_WIDTH = N_Q_HEADS * HEAD_DIM
KV_WIDTH = N_KV_HEADS * HEAD_DIM
POOL_WINDOWS = (2, 4, 8, 16)
N_POOL_GROUPS = len(POOL_WINDOWS)
POOL_WIDTH = D_MODEL - ATTN_WIDTH
POOL_GROUP_DIM = POOL_WIDTH // N_POOL_GROUPS
IN_WIDTH = ATTN_WIDTH + 2 * KV_WIDTH + POOL_WIDTH
D_FF = 4 * D_MODEL
ROPE_THETA = 10000.0
EPS = 1e-6
N_FREQ = HEAD_DIM // 4

LANES = 128
SUBLANES = 8
BF16_ROWS = 16
POOL_HALO = 8
VMEM_LIMIT_BYTES = 56 * 1024 * 1024

SEQ_TILE = 512
KEY_TILE = SEQ_TILE
Q_TILE = 128
FF_CHUNK = 1024
V_ROWS = HEAD_DIM + BF16_ROWS

F32 = jnp.float32
BF16 = jnp.bfloat16

NAT_WIDTH = KV_WIDTH + POOL_WIDTH
TR_WIDTH = ATTN_WIDTH + KV_WIDTH


def _rope_angles(seq_len):
    rows = seq_len // GRID_W
    row_ids = jnp.broadcast_to(jnp.arange(rows, dtype=F32)[:, None], (rows, GRID_W)).reshape(-1)
    col_ids = jnp.broadcast_to(jnp.arange(GRID_W, dtype=F32)[None, :], (rows, GRID_W)).reshape(-1)
    inv_freq = ROPE_THETA ** (-jnp.arange(N_FREQ, dtype=F32) / N_FREQ)
    ang = jnp.stack([row_ids[:, None] * inv_freq, col_ids[:, None] * inv_freq], axis=1)
    return jnp.cos(ang), jnp.sin(ang)


def _rope_tables(seq_len):
    cos, sin = _rope_angles(seq_len)
    zero = jnp.zeros_like(sin)
    cos_h = jnp.stack([cos, cos], axis=2).reshape(seq_len, HEAD_DIM)
    sin_up = jnp.stack([-sin, zero], axis=2).reshape(seq_len, HEAD_DIM)
    sin_dn = jnp.stack([zero, sin], axis=2).reshape(seq_len, HEAD_DIM)
    sin_sg = jnp.stack([-sin, sin], axis=2).reshape(seq_len, HEAD_DIM)
    rep = LANES // HEAD_DIM
    return (jnp.tile(cos_h, (1, rep)), jnp.tile(sin_up, (1, rep)), jnp.tile(sin_dn, (1, rep)),
            cos_h.T, sin_sg.T)


def _head_mean_matrix():
    idx = jnp.arange(LANES) // HEAD_DIM
    return jnp.where(idx[:, None] == idx[None, :], 1.0 / HEAD_DIM, 0.0).astype(BF16)


def _in_proj_kernel(x_ref, g1_ref, w_nat_ref, w_tr_ref, qg_ref, kg_ref, hm_ref,
                    cos_ref, sup_ref, sdn_ref, cos_t_ref, sin_t_ref,
                    qt_ref, k_ref, vt_ref, u_ref):
    x = x_ref[0]
    ms = jnp.mean(x * x, axis=-1, keepdims=True)
    h = (x * jax.lax.rsqrt(ms + EPS) * g1_ref[...]).astype(BF16)
    nat = jnp.dot(h, w_nat_ref[...], preferred_element_type=F32)
    tr = jax.lax.dot_general(w_tr_ref[...], h, (((1,), (1,)), ((), ())),
                             preferred_element_type=F32)

    q_gain = qg_ref[...] * (HEAD_DIM ** -0.5 * math.log2(math.e))
    cos_t, sin_t = cos_t_ref[...], sin_t_ref[...]
    for hd in range(N_Q_HEADS):
        t = tr[hd * HEAD_DIM:(hd + 1) * HEAD_DIM, :]
        tn = t * jax.lax.rsqrt(jnp.mean(t * t, axis=0, keepdims=True) + EPS) * q_gain
        partner = jnp.concatenate([tn[N_FREQ:2 * N_FREQ], tn[0:N_FREQ],
                                   tn[3 * N_FREQ:4 * N_FREQ], tn[2 * N_FREQ:3 * N_FREQ]], axis=0)
        qt_ref[0, hd] = (tn * cos_t + partner * sin_t).astype(BF16)

    row = jax.lax.broadcasted_iota(jnp.int32, (BF16_ROWS, tr.shape[1]), 0)
    ones_rows = jnp.where(row == 0, 1.0, 0.0).astype(BF16)
    for g in range(N_KV_HEADS):
        lo = ATTN_WIDTH + g * HEAD_DIM
        vt_ref[0, g, 0, 0:HEAD_DIM, :] = tr[lo:lo + HEAD_DIM, :].astype(BF16)
        vt_ref[0, g, 0, HEAD_DIM:, :] = ones_rows

    kk = nat[:, 0:KV_WIDTH]
    k2 = kk * kk
    hi = k2.astype(BF16)
    lo2 = (k2 - hi.astype(F32)).astype(BF16)
    hm = hm_ref[...]
    ms_k = (jnp.dot(hi, hm, preferred_element_type=F32)
            + jnp.dot(lo2, hm, preferred_element_type=F32))
    kn = kk * jax.lax.rsqrt(ms_k + EPS) * kg_ref[...]
    kr = (kn * cos_ref[...] + pltpu.roll(kn, LANES - N_FREQ, 1) * sup_ref[...]
          + pltpu.roll(kn, N_FREQ, 1) * sdn_ref[...])
    for g in range(N_KV_HEADS):
        k_ref[0, g] = kr[:, g * HEAD_DIM:(g + 1) * HEAD_DIM].astype(BF16)

    u_ref[0] = nat[:, KV_WIDTH:]


def _in_proj(x, g1, w_nat, w_tr, qg, kg, hm, tables):
    B, S, _ = x.shape
    T = SEQ_TILE
    assert S % T == 0
    nt = S // T
    const = lambda b, i: (0, 0)
    cos, sup, sdn, cos_t, sin_t = tables
    return pl.pallas_call(
        _in_proj_kernel,
        grid=(B, nt),
        in_specs=[
            pl.BlockSpec((1, T, D_MODEL), lambda b, i: (b, i, 0)),
            pl.BlockSpec((1, D_MODEL), const),
            pl.BlockSpec((D_MODEL, NAT_WIDTH), const),
            pl.BlockSpec((TR_WIDTH, D_MODEL), const),
            pl.BlockSpec((HEAD_DIM, 1), const),
            pl.BlockSpec((1, KV_WIDTH), const),
            pl.BlockSpec((LANES, LANES), const),
            pl.BlockSpec((T, LANES), lambda b, i: (i, 0)),
            pl.BlockSpec((T, LANES), lambda b, i: (i, 0)),
            pl.BlockSpec((T, LANES), lambda b, i: (i, 0)),
            pl.BlockSpec((HEAD_DIM, T), lambda b, i: (0, i)),
            pl.BlockSpec((HEAD_DIM, T), lambda b, i: (0, i)),
        ],
        out_specs=[
            pl.BlockSpec((1, N_Q_HEADS, HEAD_DIM, T), lambda b, i: (b, 0, 0, i)),
            pl.BlockSpec((1, N_KV_HEADS, T, HEAD_DIM), lambda b, i: (b, 0, i, 0)),
            pl.BlockSpec((1, N_KV_HEADS, 1, V_ROWS, T), lambda b, i: (b, 0, i, 0, 0)),
            pl.BlockSpec((1, T, POOL_WIDTH), lambda b, i: (b, i, 0)),
        ],
        out_shape=[
            jax.ShapeDtypeStruct((B, N_Q_HEADS, HEAD_DIM, S), BF16),
            jax.ShapeDtypeStruct((B, N_KV_HEADS, S, HEAD_DIM), BF16),
            jax.ShapeDtypeStruct((B, N_KV_HEADS, nt, V_ROWS, T), BF16),
            jax.ShapeDtypeStruct((B, S, POOL_WIDTH), F32),
        ],
        compiler_params=pltpu.CompilerParams(
            dimension_semantics=("parallel", "parallel"),
            vmem_limit_bytes=VMEM_LIMIT_BYTES),
        name="in_proj",
    )(x, g1, w_nat, w_tr, qg, kg, hm, cos, sup, sdn, cos_t, sin_t)


def _attn_kernel(qt_ref, k_ref, vt_ref, o_ref, s_ref, p_ref, m_ref, al_ref, acc_ref):
    n_kt = vt_ref.shape[2]
    tq = qt_ref.shape[3]
    qt = jnp.concatenate([qt_ref[0, h] for h in range(Q_PER_KV)], axis=1)

    m_ref[...] = jnp.full(m_ref.shape, -1e30, F32)
    acc_ref[...] = jnp.zeros(acc_ref.shape, F32)

    def scores(j):
        s_ref[j % 2] = jnp.dot(k_ref[0, 0, j * KEY_TILE:(j + 1) * KEY_TILE, :], qt,
                               preferred_element_type=F32)

    def softmax(j):
        s = s_ref[j % 2]
        m_old = m_ref[...]
        m_new = jnp.maximum(m_old, jnp.max(s, axis=0, keepdims=True))
        al_ref[j % 2] = jnp.exp2(m_old - m_new)
        m_ref[...] = m_new
        p_ref[j % 2] = jnp.exp2(s - m_new).astype(BF16)

    def accumulate(j):
        pv = jnp.dot(vt_ref[0, 0, j], p_ref[j % 2], preferred_element_type=F32)
        acc_ref[...] = al_ref[j % 2] * acc_ref[...] + pv

    for t in range(n_kt + 2):
        if t < n_kt:
            scores(t)
        if 1 <= t <= n_kt:
            softmax(t - 1)
        if t >= 2:
            accumulate(t - 2)

    acc = acc_ref[...]
    out_t = acc[0:HEAD_DIM] / acc[HEAD_DIM:HEAD_DIM + 1]
    o_ref[0] = jnp.concatenate([out_t[:, h * tq:(h + 1) * tq].T for h in range(Q_PER_KV)],
                               axis=1).astype(o_ref.dtype)


def _attention(qt, k, vt):
    B, _, _, S = qt.shape
    n_kt = vt.shape[2]
    tq = Q_TILE
    lanes = Q_PER_KV * tq
    return pl.pallas_call(
        _attn_kernel,
        grid=(B, N_KV_HEADS, S // tq),
        in_specs=[
            pl.BlockSpec((1, Q_PER_KV, HEAD_DIM, tq), lambda b, g, i: (b, g, 0, i)),
            pl.BlockSpec((1, 1, S, HEAD_DIM), lambda b, g, i: (b, g, 0, 0)),
            pl.BlockSpec((1, 1, n_kt, V_ROWS, KEY_TILE), lambda b, g, i: (b, g, 0, 0, 0)),
        ],
        out_specs=pl.BlockSpec((1, tq, Q_PER_KV * HEAD_DIM), lambda b, g, i: (b, i, g)),
        out_shape=jax.ShapeDtypeStruct((B, S, ATTN_WIDTH), BF16),
        scratch_shapes=[
            pltpu.VMEM((2, KEY_TILE, lanes), F32),
            pltpu.VMEM((2, KEY_TILE, lanes), BF16),
            pltpu.VMEM((1, lanes), F32),
            pltpu.VMEM((2, 1, lanes), F32),
            pltpu.VMEM((V_ROWS, lanes), F32),
        ],
        compiler_params=pltpu.CompilerParams(
            dimension_semantics=("parallel", "parallel", "arbitrary"),
            vmem_limit_bytes=VMEM_LIMIT_BYTES),
        name="attention",
    )(qt, k, vt)


def _mix_mlp_kernel(x_ref, a_ref, u_ref, up_ref, un_ref, wp_ref, ps_ref, wo_ref, g2_ref,
                    wu_ref, wd_ref, y_ref, ue_ref):
    i = pl.program_id(1)
    nt = pl.num_programs(1)
    T = x_ref.shape[1]
    S = T * nt

    u = u_ref[0]
    ue_ref[POOL_HALO:POOL_HALO + T, :] = u
    ue_ref[0:POOL_HALO, :] = jnp.where(i > 0, up_ref[0], 0.0)
    ue_ref[POOL_HALO + T:, :] = jnp.where(i < nt - 1, un_ref[0], 0.0)

    t = i * T + jax.lax.broadcasted_iota(jnp.int32, (T, 1), 0)
    mix = jnp.dot(a_ref[0], wo_ref[0:ATTN_WIDTH, :], preferred_element_type=F32)
    for g, win in enumerate(POOL_WINDOWS):
        sl = slice(g * POOL_GROUP_DIM, (g + 1) * POOL_GROUP_DIM)
        half = win // 2
        wsum = ue_ref[POOL_HALO - half:POOL_HALO - half + T, sl]
        for j in range(1 - half, half):
            wsum = wsum + ue_ref[POOL_HALO + j:POOL_HALO + j + T, sl]
        cnt = (jnp.minimum(t + half, S) - jnp.maximum(t - half, 0)).astype(F32)
        d = wsum / cnt - u[:, sl]
        z = jnp.dot(d.astype(BF16), wp_ref[g], preferred_element_type=F32) * ps_ref[:, sl]
        mix = mix + jnp.dot(z.astype(BF16), wo_ref[ATTN_WIDTH + g * POOL_GROUP_DIM:
                                                   ATTN_WIDTH + (g + 1) * POOL_GROUP_DIM, :],
                            preferred_element_type=F32)

    x1 = x_ref[0] + mix
    ms = jnp.mean(x1 * x1, axis=-1, keepdims=True)
    h2 = (x1 * jax.lax.rsqrt(ms + EPS) * g2_ref[...]).astype(BF16)
    acc = x1
    for c in range(D_FF // FF_CHUNK):
        sl = slice(c * FF_CHUNK, (c + 1) * FF_CHUNK)
        f = jnp.dot(h2, wu_ref[:, sl], preferred_element_type=F32)
        f = jnp.square(jnp.maximum(f, 0.0)).astype(BF16)
        acc = acc + jnp.dot(f, wd_ref[sl, :], preferred_element_type=F32)
    y_ref[0] = acc


def _mix_mlp(x, a, u, w_pool, pool_scale, w_out, g2, w_up, w_down):
    B, S, _ = x.shape
    T = SEQ_TILE
    nt = S // T
    hb = T // POOL_HALO
    n_hb = S // POOL_HALO
    const2 = lambda b, i: (0, 0)
    resident = functools.partial(pl.BlockSpec, pipeline_mode=pl.Buffered(1))
    return pl.pallas_call(
        _mix_mlp_kernel,
        grid=(B, nt),
        in_specs=[
            pl.BlockSpec((1, T, D_MODEL), lambda b, i: (b, i, 0)),
            pl.BlockSpec((1, T, ATTN_WIDTH), lambda b, i: (b, i, 0)),
            pl.BlockSpec((1, T, POOL_WIDTH), lambda b, i: (b, i, 0)),
            pl.BlockSpec((1, POOL_HALO, POOL_WIDTH),
                         lambda b, i: (b, jnp.maximum(i * hb - 1, 0), 0)),
            pl.BlockSpec((1, POOL_HALO, POOL_WIDTH),
                         lambda b, i: (b, jnp.minimum((i + 1) * hb, n_hb - 1), 0)),
            resident((N_POOL_GROUPS, POOL_GROUP_DIM, POOL_GROUP_DIM), lambda b, i: (0, 0, 0)),
            pl.BlockSpec((1, POOL_WIDTH), const2),
            resident((D_MODEL, D_MODEL), const2),
            pl.BlockSpec((1, D_MODEL), const2),
            resident((D_MODEL, D_FF), const2),
            resident((D_FF, D_MODEL), const2),
        ],
        out_specs=pl.BlockSpec((1, T, D_MODEL), lambda b, i: (b, i, 0)),
        out_shape=jax.ShapeDtypeStruct((B, S, D_MODEL), F32),
        scratch_shapes=[pltpu.VMEM((T + 2 * POOL_HALO, POOL_WIDTH), F32)],
        compiler_params=pltpu.CompilerParams(
            dimension_semantics=("parallel", "parallel"),
            vmem_limit_bytes=VMEM_LIMIT_BYTES),
        name="mix_mlp",
    )(x, a, u, u, u, w_pool, pool_scale, w_out, g2, w_up, w_down)


def _layer(x, g1, w_nat, w_tr, qg, kg, w_pool, pool_scale, w_out, g2, w_up, w_down, hm, tables):
    qt, k, vt, u = _in_proj(x, g1, w_nat, w_tr, qg, kg, hm, tables)
    a = _attention(qt, k, vt)
    return _mix_mlp(x, a, u, w_pool, pool_scale, w_out, g2, w_up, w_down)


def _trunk(x, layers, hm):
    tables = _rope_tables(x.shape[1])
    for params in layers:
        x = _layer(x, *params, hm, tables)
    return x


def kernel(x_prompt, x_sample, norm1_g, w_in, q_norm_g, k_norm_g, w_pool, pool_scale, w_out,
           norm2_g, w_up, w_down):
    depth = w_in.shape[0]
    q_end, k_end, v_end = ATTN_WIDTH, ATTN_WIDTH + KV_WIDTH, ATTN_WIDTH + 2 * KV_WIDTH
    layers = []
    for l in range(depth):
        w = w_in[l].astype(BF16)
        layers.append((
            norm1_g[l][None, :],
            jnp.concatenate([w[:, q_end:k_end], w[:, v_end:]], axis=1),
            jnp.concatenate([w[:, :q_end], w[:, k_end:v_end]], axis=1).T,
            q_norm_g[l][:, None],
            jnp.tile(k_norm_g[l], N_KV_HEADS)[None, :],
            w_pool[l].astype(BF16),
            pool_scale[l][None, :],
            w_out[l].astype(BF16),
            norm2_g[l][None, :],
            w_up[l].astype(BF16),
            w_down[l].astype(BF16),
        ))
    hm = _head_mean_matrix()
    return (_trunk(x_prompt, layers, hm), _trunk(x_sample, layers, hm))
```

```python
import functools
import math

import jax
import jax.numpy as jnp
from jax.experimental import pallas as pl
from jax.experimental.pallas import tpu as pltpu

D_MODEL = 1024
GRID_W = 64
HEAD_DIM = 64
N_Q_HEADS = 8
N_KV_HEADS = 2
Q_PER_KV = N_Q_HEADS // N_KV_HEADS
ATTN_WIDTH = N_Q_HEADS * HEAD_DIM
KV_WIDTH = N_KV_HEADS * HEAD_DIM
POOL_WINDOWS = (2, 4, 8, 16)
N_POOL_GROUPS = len(POOL_WINDOWS)
POOL_WIDTH = D_MODEL - ATTN_WIDTH
POOL_GROUP_DIM = POOL_WIDTH // N_POOL_GROUPS
IN_WIDTH = ATTN_WIDTH + 2 * KV_WIDTH + POOL_WIDTH
D_FF = 4 * D_MODEL
ROPE_THETA = 10000.0
EPS = 1e-6
N_FREQ = HEAD_DIM // 4

LANES = 128
SUBLANES = 8
BF16_ROWS = 16
POOL_HALO = 8
VMEM_LIMIT_BYTES = 56 * 1024 * 1024

SEQ_TILE = 512
KEY_TILE = SEQ_TILE
Q_TILE = 128
FF_CHUNK = 1024
V_ROWS = HEAD_DIM + BF16_ROWS
MAX_BOUND_SHIFT = 30.0
BOUND_MARGIN = 1.02
STREAM_KEYS = 256
STREAM_AHEAD = 2

F32 = jnp.float32
BF16 = jnp.bfloat16

NAT_WIDTH = KV_WIDTH + POOL_WIDTH
TR_WIDTH = ATTN_WIDTH + KV_WIDTH


def _rope_angles(seq_len):
    rows = seq_len // GRID_W
    row_ids = jnp.broadcast_to(jnp.arange(rows, dtype=F32)[:, None], (rows, GRID_W)).reshape(-1)
    col_ids = jnp.broadcast_to(jnp.arange(GRID_W, dtype=F32)[None, :], (rows, GRID_W)).reshape(-1)
    inv_freq = ROPE_THETA ** (-jnp.arange(N_FREQ, dtype=F32) / N_FREQ)
    ang = jnp.stack([row_ids[:, None] * inv_freq, col_ids[:, None] * inv_freq], axis=1)
    return jnp.cos(ang), jnp.sin(ang)


def _rope_tables(seq_len):
    cos, sin = _rope_angles(seq_len)
    zero = jnp.zeros_like(sin)
    cos_h = jnp.stack([cos, cos], axis=2).reshape(seq_len, HEAD_DIM)
    sin_up = jnp.stack([-sin, zero], axis=2).reshape(seq_len, HEAD_DIM)
    sin_dn = jnp.stack([zero, sin], axis=2).reshape(seq_len, HEAD_DIM)
    sin_sg = jnp.stack([-sin, sin], axis=2).reshape(seq_len, HEAD_DIM)
    rep = LANES // HEAD_DIM
    return (jnp.tile(cos_h, (1, rep)), jnp.tile(sin_up, (1, rep)), jnp.tile(sin_dn, (1, rep)),
            cos_h.T, sin_sg.T)


def _head_mean_matrix():
    idx = jnp.arange(LANES) // HEAD_DIM
    return jnp.where(idx[:, None] == idx[None, :], 1.0 / HEAD_DIM, 0.0).astype(BF16)


def _in_proj_kernel(x_ref, g1_ref, w_nat_ref, w_tr_ref, qg_ref, kg_ref, hm_ref,
                    cos_ref, sup_ref, sdn_ref, cos_t_ref, sin_t_ref,
                    qt_ref, qn_ref, q2_ref, k_ref, k2_ref, vt_ref, u_ref):
    x = x_ref[0]
    ms = jnp.mean(x * x, axis=-1, keepdims=True)
    h = (x * jax.lax.rsqrt(ms + EPS) * g1_ref[...]).astype(BF16)
    nat = jnp.dot(h, w_nat_ref[...], preferred_element_type=F32)
    tr = jax.lax.dot_general(w_tr_ref[...], h, (((1,), (1,)), ((), ())),
                             preferred_element_type=F32)

    q_gain = qg_ref[...] * (HEAD_DIM ** -0.5 * math.log2(math.e))
    cos_t, sin_t = cos_t_ref[...], sin_t_ref[...]
    q2_rows = []
    for hd in range(N_Q_HEADS):
        t = tr[hd * HEAD_DIM:(hd + 1) * HEAD_DIM, :]
        tn = t * jax.lax.rsqrt(jnp.mean(t * t, axis=0, keepdims=True) + EPS) * q_gain
        partner = jnp.concatenate([tn[N_FREQ:2 * N_FREQ], tn[0:N_FREQ],
                                   tn[3 * N_FREQ:4 * N_FREQ], tn[2 * N_FREQ:3 * N_FREQ]], axis=0)
        qt_ref[0, hd] = (tn * cos_t + partner * sin_t).astype(BF16)
        qn2 = jnp.sum(tn * tn, axis=0, keepdims=True)
        qn_ref[0, hd] = qn2
        q2_rows.append(jnp.broadcast_to(jnp.max(qn2, axis=1, keepdims=True), (1, LANES)))
    q2_ref[0, 0] = jnp.concatenate(q2_rows, axis=0)

    row = jax.lax.broadcasted_iota(jnp.int32, (BF16_ROWS, tr.shape[1]), 0)
    ones_rows = jnp.where(row == 0, 1.0, 0.0).astype(BF16)
    for g in range(N_KV_HEADS):
        lo = ATTN_WIDTH + g * HEAD_DIM
        vt_ref[0, g, 0, 0:HEAD_DIM, :] = tr[lo:lo + HEAD_DIM, :].astype(BF16)
        vt_ref[0, g, 0, HEAD_DIM:, :] = ones_rows

    kk = nat[:, 0:KV_WIDTH]
    k2 = kk * kk
    hi = k2.astype(BF16)
    lo2 = (k2 - hi.astype(F32)).astype(BF16)
    hm = hm_ref[...]
    ms_k = (jnp.dot(hi, hm, preferred_element_type=F32)
            + jnp.dot(lo2, hm, preferred_element_type=F32))
    kn = kk * jax.lax.rsqrt(ms_k + EPS) * kg_ref[...]
    kr = (kn * cos_ref[...] + pltpu.roll(kn, LANES - N_FREQ, 1) * sup_ref[...]
          + pltpu.roll(kn, N_FREQ, 1) * sdn_ref[...])
    for g in range(N_KV_HEADS):
        k_ref[0, g] = kr[:, g * HEAD_DIM:(g + 1) * HEAD_DIM].astype(BF16)
    kn2 = jnp.dot((kn * kn).astype(BF16), hm, preferred_element_type=F32) * HEAD_DIM
    k2_ref[0, 0] = jnp.broadcast_to(jnp.max(kn2, axis=0, keepdims=True), (SUBLANES, LANES))

    u_ref[0] = nat[:, KV_WIDTH:]


def _in_proj(x, g1, w_nat, w_tr, qg, kg, hm, tables):
    B, S, _ = x.shape
    T = SEQ_TILE
    assert S % T == 0
    nt = S // T
    const = lambda b, i: (0, 0)
    cos, sup, sdn, cos_t, sin_t = tables
    return pl.pallas_call(
        _in_proj_kernel,
        grid=(B, nt),
        in_specs=[
            pl.BlockSpec((1, T, D_MODEL), lambda b, i: (b, i, 0)),
            pl.BlockSpec((1, D_MODEL), const),
            pl.BlockSpec((D_MODEL, NAT_WIDTH), const),
            pl.BlockSpec((TR_WIDTH, D_MODEL), const),
            pl.BlockSpec((HEAD_DIM, 1), const),
            pl.BlockSpec((1, KV_WIDTH), const),
            pl.BlockSpec((LANES, LANES), const),
            pl.BlockSpec((T, LANES), lambda b, i: (i, 0)),
            pl.BlockSpec((T, LANES), lambda b, i: (i, 0)),
            pl.BlockSpec((T, LANES), lambda b, i: (i, 0)),
            pl.BlockSpec((HEAD_DIM, T), lambda b, i: (0, i)),
            pl.BlockSpec((HEAD_DIM, T), lambda b, i: (0, i)),
        ],
        out_specs=[
            pl.BlockSpec((1, N_Q_HEADS, HEAD_DIM, T), lambda b, i: (b, 0, 0, i)),
            pl.BlockSpec((1, N_Q_HEADS, 1, T), lambda b, i: (b, 0, 0, i)),
            pl.BlockSpec((1, 1, N_Q_HEADS, LANES), lambda b, i: (b, i, 0, 0)),
            pl.BlockSpec((1, N_KV_HEADS, T, HEAD_DIM), lambda b, i: (b, 0, i, 0)),
            pl.BlockSpec((1, 1, SUBLANES, LANES), lambda b, i: (b, i, 0, 0)),
            pl.BlockSpec((1, N_KV_HEADS, 1, V_ROWS, T), lambda b, i: (b, 0, i, 0, 0)),
            pl.BlockSpec((1, T, POOL_WIDTH), lambda b, i: (b, i, 0)),
        ],
        out_shape=[
            jax.ShapeDtypeStruct((B, N_Q_HEADS, HEAD_DIM, S), BF16),
            jax.ShapeDtypeStruct((B, N_Q_HEADS, 1, S), F32),
            jax.ShapeDtypeStruct((B, nt, N_Q_HEADS, LANES), F32),
            jax.ShapeDtypeStruct((B, N_KV_HEADS, S, HEAD_DIM), BF16),
            jax.ShapeDtypeStruct((B, nt, SUBLANES, LANES), F32),
            jax.ShapeDtypeStruct((B, N_KV_HEADS, nt, V_ROWS, T), BF16),
            jax.ShapeDtypeStruct((B, S, POOL_WIDTH), F32),
        ],
        compiler_params=pltpu.CompilerParams(
            dimension_semantics=("parallel", "parallel"),
            vmem_limit_bytes=VMEM_LIMIT_BYTES),
        name="in_proj",
    )(x, g1, w_nat, w_tr, qg, kg, hm, cos, sup, sdn, cos_t, sin_t)


def _attn_kernel(safe_ref, qt_ref, qn_ref, k_ref, k2_ref, vt_ref, o_ref,
                 s_ref, p_ref, m_ref, al_ref, acc_ref):
    n_kt = vt_ref.shape[2]
    tq = qt_ref.shape[3]
    qt = jnp.concatenate([qt_ref[0, h] for h in range(Q_PER_KV)], axis=1)

    def key_tile(j):
        return k_ref[0, 0, j * KEY_TILE:(j + 1) * KEY_TILE, :]

    def finish(acc):
        out_t = acc[0:HEAD_DIM] / acc[HEAD_DIM:HEAD_DIM + 1]
        o_ref[0] = jnp.concatenate([out_t[:, h * tq:(h + 1) * tq].T for h in range(Q_PER_KV)],
                                   axis=1).astype(o_ref.dtype)

    qn2 = jnp.concatenate([qn_ref[0, h] for h in range(Q_PER_KV)], axis=1)
    k2_max = jnp.concatenate([k2_ref[0, 0]] * (Q_PER_KV * tq // LANES), axis=1)
    bound = jnp.sqrt(qn2 * k2_max) * BOUND_MARGIN

    def shifted_by_bound():
        per_tile = KEY_TILE // STREAM_KEYS
        n_blk = n_kt * per_tile

        def scores(b):
            return jnp.dot(k_ref[0, 0, b * STREAM_KEYS:(b + 1) * STREAM_KEYS, :], qt,
                           preferred_element_type=F32)

        acc = jnp.zeros((V_ROWS, Q_PER_KV * tq), F32)
        pending = [scores(b) for b in range(min(STREAM_AHEAD, n_blk))]
        for b in range(n_blk):
            if b + STREAM_AHEAD < n_blk:
                pending.append(scores(b + STREAM_AHEAD))
            p = jnp.exp2(pending.pop(0) - bound).astype(BF16)
            j, sub = divmod(b, per_tile)
            vt = vt_ref[0, 0, j, :, sub * STREAM_KEYS:(sub + 1) * STREAM_KEYS]
            acc = acc + jnp.dot(vt, p, preferred_element_type=F32)
        finish(acc)

    def online_max():
        m_ref[...] = jnp.full(m_ref.shape, -1e30, F32)
        acc_ref[...] = jnp.zeros(acc_ref.shape, F32)

        def scores(j):
            s_ref[j % 2] = jnp.dot(key_tile(j), qt, preferred_element_type=F32)

        def softmax(j):
            s = s_ref[j % 2]
            m_old = m_ref[...]
            m_new = jnp.maximum(m_old, jnp.max(s, axis=0, keepdims=True))
            al_ref[j % 2] = jnp.exp2(m_old - m_new)
            m_ref[...] = m_new
            p_ref[j % 2] = jnp.exp2(s - m_new).astype(BF16)

        def accumulate(j):
            pv = jnp.dot(vt_ref[0, 0, j], p_ref[j % 2], preferred_element_type=F32)
            acc_ref[...] = al_ref[j % 2] * acc_ref[...] + pv

        for t in range(n_kt + 2):
            if t < n_kt:
                scores(t)
            if 1 <= t <= n_kt:
                softmax(t - 1)
            if t >= 2:
                accumulate(t - 2)
        finish(acc_ref[...])

    jax.lax.cond(safe_ref[pl.program_id(0), pl.program_id(1)] != 0, shifted_by_bound, online_max)


def _attention(qt, qn2, q2_tiles, k, k2_tiles, vt):
    B, _, _, S = qt.shape
    n_kt = vt.shape[2]
    tq = Q_TILE
    lanes = Q_PER_KV * tq

    q2_max = jnp.max(q2_tiles[:, :, :, 0], axis=1).reshape(B, N_KV_HEADS, Q_PER_KV).max(axis=-1)
    k2_max = jnp.max(k2_tiles[:, :, 0, ::HEAD_DIM], axis=1)
    safe = (jnp.sqrt(q2_max * k2_max) * BOUND_MARGIN <= MAX_BOUND_SHIFT).astype(jnp.int32)
    k2_lanes = jnp.broadcast_to(k2_max[:, :, None, None], (B, N_KV_HEADS, 1, LANES))

    grid_spec = pltpu.PrefetchScalarGridSpec(
        num_scalar_prefetch=1,
        grid=(B, N_KV_HEADS, S // tq),
        in_specs=[
            pl.BlockSpec((1, Q_PER_KV, HEAD_DIM, tq), lambda b, g, i, safe: (b, g, 0, i)),
            pl.BlockSpec((1, Q_PER_KV, 1, tq), lambda b, g, i, safe: (b, g, 0, i)),
            pl.BlockSpec((1, 1, S, HEAD_DIM), lambda b, g, i, safe: (b, g, 0, 0)),
            pl.BlockSpec((1, 1, 1, LANES), lambda b, g, i, safe: (b, g, 0, 0)),
            pl.BlockSpec((1, 1, n_kt, V_ROWS, KEY_TILE), lambda b, g, i, safe: (b, g, 0, 0, 0)),
        ],
        out_specs=pl.BlockSpec((1, tq, Q_PER_KV * HEAD_DIM), lambda b, g, i, safe: (b, i, g)),
        scratch_shapes=[
            pltpu.VMEM((2, KEY_TILE, lanes), F32),
            pltpu.VMEM((2, KEY_TILE, lanes), BF16),
            pltpu.VMEM((1, lanes), F32),
            pltpu.VMEM((2, 1, lanes), F32),
            pltpu.VMEM((V_ROWS, lanes), F32),
        ],
    )
    return pl.pallas_call(
        _attn_kernel,
        grid_spec=grid_spec,
        out_shape=jax.ShapeDtypeStruct((B, S, ATTN_WIDTH), BF16),
        compiler_params=pltpu.CompilerParams(
            dimension_semantics=("parallel", "parallel", "arbitrary"),
            vmem_limit_bytes=VMEM_LIMIT_BYTES),
        name="attention",
    )(safe, qt, qn2, k, k2_lanes, vt)


def _mix_mlp_kernel(x_ref, a_ref, u_ref, up_ref, un_ref, wp_ref, ps_ref, wo_ref, g2_ref,
                    wu_ref, wd_ref, y_ref, ue_ref):
    i = pl.program_id(1)
    nt = pl.num_programs(1)
    T = x_ref.shape[1]
    S = T * nt

    u = u_ref[0]
    ue_ref[POOL_HALO:POOL_HALO + T, :] = u
    ue_ref[0:POOL_HALO, :] = jnp.where(i > 0, up_ref[0], 0.0)
    ue_ref[POOL_HALO + T:, :] = jnp.where(i < nt - 1, un_ref[0], 0.0)

    t = i * T + jax.lax.broadcasted_iota(jnp.int32, (T, 1), 0)
    mix = jnp.dot(a_ref[0], wo_ref[0:ATTN_WIDTH, :], preferred_element_type=F32)
    for g, win in enumerate(POOL_WINDOWS):
        sl = slice(g * POOL_GROUP_DIM, (g + 1) * POOL_GROUP_DIM)
        half = win // 2
        wsum = ue_ref[POOL_HALO - half:POOL_HALO - half + T, sl]
        for j in range(1 - half, half):
            wsum = wsum + ue_ref[POOL_HALO + j:POOL_HALO + j + T, sl]
        cnt = (jnp.minimum(t + half, S) - jnp.maximum(t - half, 0)).astype(F32)
        d = wsum / cnt - u[:, sl]
        z = jnp.dot(d.astype(BF16), wp_ref[g], preferred_element_type=F32) * ps_ref[:, sl]
        mix = mix + jnp.dot(z.astype(BF16), wo_ref[ATTN_WIDTH + g * POOL_GROUP_DIM:
                                                   ATTN_WIDTH + (g + 1) * POOL_GROUP_DIM, :],
                            preferred_element_type=F32)

    x1 = x_ref[0] + mix
    ms = jnp.mean(x1 * x1, axis=-1, keepdims=True)
    h2 = (x1 * jax.lax.rsqrt(ms + EPS) * g2_ref[...]).astype(BF16)
    acc = x1
    for c in range(D_FF // FF_CHUNK):
        sl = slice(c * FF_CHUNK, (c + 1) * FF_CHUNK)
        f = jnp.dot(h2, wu_ref[:, sl], preferred_element_type=F32)
        f = jnp.square(jnp.maximum(f, 0.0)).astype(BF16)
        acc = acc + jnp.dot(f, wd_ref[sl, :], preferred_element_type=F32)
    y_ref[0] = acc


def _mix_mlp(x, a, u, w_pool, pool_scale, w_out, g2, w_up, w_down):
    B, S, _ = x.shape
    T = SEQ_TILE
    nt = S // T
    hb = T // POOL_HALO
    n_hb = S // POOL_HALO
    const2 = lambda b, i: (0, 0)
    resident = functools.partial(pl.BlockSpec, pipeline_mode=pl.Buffered(1))
    return pl.pallas_call(
        _mix_mlp_kernel,
        grid=(B, nt),
        in_specs=[
            pl.BlockSpec((1, T, D_MODEL), lambda b, i: (b, i, 0)),
            pl.BlockSpec((1, T, ATTN_WIDTH), lambda b, i: (b, i, 0)),
            pl.BlockSpec((1, T, POOL_WIDTH), lambda b, i: (b, i, 0)),
            pl.BlockSpec((1, POOL_HALO, POOL_WIDTH),
                         lambda b, i: (b, jnp.maximum(i * hb - 1, 0), 0)),
            pl.BlockSpec((1, POOL_HALO, POOL_WIDTH),
                         lambda b, i: (b, jnp.minimum((i + 1) * hb, n_hb - 1), 0)),
            resident((N_POOL_GROUPS, POOL_GROUP_DIM, POOL_GROUP_DIM), lambda b, i: (0, 0, 0)),
            pl.BlockSpec((1, POOL_WIDTH), const2),
            resident((D_MODEL, D_MODEL), const2),
            pl.BlockSpec((1, D_MODEL), const2),
            resident((D_MODEL, D_FF), const2),
            resident((D_FF, D_MODEL), const2),
        ],
        out_specs=pl.BlockSpec((1, T, D_MODEL), lambda b, i: (b, i, 0)),
        out_shape=jax.ShapeDtypeStruct((B, S, D_MODEL), F32),
        scratch_shapes=[pltpu.VMEM((T + 2 * POOL_HALO, POOL_WIDTH), F32)],
        compiler_params=pltpu.CompilerParams(
            dimension_semantics=("parallel", "parallel"),
            vmem_limit_bytes=VMEM_LIMIT_BYTES),
        name="mix_mlp",
    )(x, a, u, u, u, w_pool, pool_scale, w_out, g2, w_up, w_down)


def _layer(x, g1, w_nat, w_tr, qg, kg, w_pool, pool_scale, w_out, g2, w_up, w_down, hm, tables):
    qt, qn2, q2_tiles, k, k2_tiles, vt, u = _in_proj(x, g1, w_nat, w_tr, qg, kg, hm, tables)
    a = _attention(qt, qn2, q2_tiles, k, k2_tiles, vt)
    return _mix_mlp(x, a, u, w_pool, pool_scale, w_out, g2, w_up, w_down)


def _trunk(x, layers, hm):
    tables = _rope_tables(x.shape[1])
    for params in layers:
        x = _layer(x, *params, hm, tables)
    return x


def kernel(x_prompt, x_sample, norm1_g, w_in, q_norm_g, k_norm_g, w_pool, pool_scale, w_out,
           norm2_g, w_up, w_down):
    depth = w_in.shape[0]
    q_end, k_end, v_end = ATTN_WIDTH, ATTN_WIDTH + KV_WIDTH, ATTN_WIDTH + 2 * KV_WIDTH
    layers = []
    for l in range(depth):
        w = w_in[l].astype(BF16)
        layers.append((
            norm1_g[l][None, :],
            jnp.concatenate([w[:, q_end:k_end], w[:, v_end:]], axis=1),
            jnp.concatenate([w[:, :q_end], w[:, k_end:v_end]], axis=1).T,
            q_norm_g[l][:, None],
            jnp.tile(k_norm_g[l], N_KV_HEADS)[None, :],
            w_pool[l].astype(BF16),
            pool_scale[l][None, :],
            w_out[l].astype(BF16),
            norm2_g[l][None, :],
            w_up[l].astype(BF16),
            w_down[l].astype(BF16),
        ))
    hm = _head_mean_matrix()
    return (_trunk(x_prompt, layers, hm), _trunk(x_sample, layers, hm))
```

```python
import functools
import math

import jax
import jax.numpy as jnp
from jax.experimental import pallas as pl
from jax.experimental.pallas import tpu as pltpu

D_MODEL = 1024
GRID_W = 64
HEAD_DIM = 64
N_Q_HEADS = 8
N_KV_HEADS = 2
Q_PER_KV = N_Q_HEADS // N_KV_HEADS
ATTN_WIDTH = N_Q_HEADS * HEAD_DIM
KV_WIDTH = N_KV_HEADS * HEAD_DIM
POOL_WINDOWS = (2, 4, 8, 16)
N_POOL_GROUPS = len(POOL_WINDOWS)
POOL_WIDTH = D_MODEL - ATTN_WIDTH
POOL_GROUP_DIM = POOL_WIDTH // N_POOL_GROUPS
IN_WIDTH = ATTN_WIDTH + 2 * KV_WIDTH + POOL_WIDTH
D_FF = 4 * D_MODEL
ROPE_THETA = 10000.0
EPS = 1e-6
N_FREQ = HEAD_DIM // 4

LANES = 128
SUBLANES = 8
BF16_ROWS = 16
POOL_HALO = 8
VMEM_LIMIT_BYTES = 56 * 1024 * 1024

SEQ_TILE = 512
KEY_TILE = SEQ_TILE
Q_SUB = 128
Q_TILE = 512
FF_CHUNK = 1024
V_ROWS = HEAD_DIM + BF16_ROWS
MAX_BOUND_SHIFT = 30.0
BOUND_MARGIN = 1.02
STREAM_KEYS = 256
STREAM_AHEAD = 2

F32 = jnp.float32
BF16 = jnp.bfloat16

NAT_WIDTH = KV_WIDTH + POOL_WIDTH
TR_WIDTH = ATTN_WIDTH + KV_WIDTH


def _rope_angles(seq_len):
    rows = seq_len // GRID_W
    row_ids = jnp.broadcast_to(jnp.arange(rows, dtype=F32)[:, None], (rows, GRID_W)).reshape(-1)
    col_ids = jnp.broadcast_to(jnp.arange(GRID_W, dtype=F32)[None, :], (rows, GRID_W)).reshape(-1)
    inv_freq = ROPE_THETA ** (-jnp.arange(N_FREQ, dtype=F32) / N_FREQ)
    ang = jnp.stack([row_ids[:, None] * inv_freq, col_ids[:, None] * inv_freq], axis=1)
    return jnp.cos(ang), jnp.sin(ang)


def _rope_tables(seq_len):
    cos, sin = _rope_angles(seq_len)
    zero = jnp.zeros_like(sin)
    cos_h = jnp.stack([cos, cos], axis=2).reshape(seq_len, HEAD_DIM)
    sin_up = jnp.stack([-sin, zero], axis=2).reshape(seq_len, HEAD_DIM)
    sin_dn = jnp.stack([zero, sin], axis=2).reshape(seq_len, HEAD_DIM)
    sin_sg = jnp.stack([-sin, sin], axis=2).reshape(seq_len, HEAD_DIM)
    rep = LANES // HEAD_DIM
    return (jnp.tile(cos_h, (1, rep)), jnp.tile(sin_up, (1, rep)), jnp.tile(sin_dn, (1, rep)),
            cos_h.T, sin_sg.T)


def _head_mean_matrix():
    idx = jnp.arange(LANES) // HEAD_DIM
    return jnp.where(idx[:, None] == idx[None, :], 1.0 / HEAD_DIM, 0.0).astype(BF16)


def _in_proj_kernel(x_ref, g1_ref, w_nat_ref, w_tr_ref, qg_ref, kg_ref, hm_ref,
                    cos_ref, sup_ref, sdn_ref, cos_t_ref, sin_t_ref,
                    qt_ref, qn_ref, q2_ref, k_ref, k2_ref, vt_ref, u_ref):
    x = x_ref[0]
    ms = jnp.mean(x * x, axis=-1, keepdims=True)
    h = (x * jax.lax.rsqrt(ms + EPS) * g1_ref[...]).astype(BF16)
    nat = jnp.dot(h, w_nat_ref[...], preferred_element_type=F32)
    tr = jax.lax.dot_general(w_tr_ref[...], h, (((1,), (1,)), ((), ())),
                             preferred_element_type=F32)

    q_gain = qg_ref[...] * (HEAD_DIM ** -0.5 * math.log2(math.e))
    cos_t, sin_t = cos_t_ref[...], sin_t_ref[...]
    q2_rows = []
    for hd in range(N_Q_HEADS):
        t = tr[hd * HEAD_DIM:(hd + 1) * HEAD_DIM, :]
        tn = t * jax.lax.rsqrt(jnp.mean(t * t, axis=0, keepdims=True) + EPS) * q_gain
        partner = jnp.concatenate([tn[N_FREQ:2 * N_FREQ], tn[0:N_FREQ],
                                   tn[3 * N_FREQ:4 * N_FREQ], tn[2 * N_FREQ:3 * N_FREQ]], axis=0)
        qr = (tn * cos_t + partner * sin_t).astype(BF16)
        qn2 = jnp.sum(tn * tn, axis=0, keepdims=True)
        for c in range(tr.shape[1] // Q_SUB):
            qt_ref[0, hd, c] = qr[:, c * Q_SUB:(c + 1) * Q_SUB]
            qn_ref[0, hd, c] = qn2[:, c * Q_SUB:(c + 1) * Q_SUB]
        q2_rows.append(jnp.broadcast_to(jnp.max(qn2, axis=1, keepdims=True), (1, LANES)))
    q2_ref[0, 0] = jnp.concatenate(q2_rows, axis=0)

    row = jax.lax.broadcasted_iota(jnp.int32, (BF16_ROWS, tr.shape[1]), 0)
    ones_rows = jnp.where(row == 0, 1.0, 0.0).astype(BF16)
    for g in range(N_KV_HEADS):
        lo = ATTN_WIDTH + g * HEAD_DIM
        vt_ref[0, g, 0, 0:HEAD_DIM, :] = tr[lo:lo + HEAD_DIM, :].astype(BF16)
        vt_ref[0, g, 0, HEAD_DIM:, :] = ones_rows

    kk = nat[:, 0:KV_WIDTH]
    k2 = kk * kk
    hi = k2.astype(BF16)
    lo2 = (k2 - hi.astype(F32)).astype(BF16)
    hm = hm_ref[...]
    ms_k = (jnp.dot(hi, hm, preferred_element_type=F32)
            + jnp.dot(lo2, hm, preferred_element_type=F32))
    kn = kk * jax.lax.rsqrt(ms_k + EPS) * kg_ref[...]
    kr = (kn * cos_ref[...] + pltpu.roll(kn, LANES - N_FREQ, 1) * sup_ref[...]
          + pltpu.roll(kn, N_FREQ, 1) * sdn_ref[...])
    for g in range(N_KV_HEADS):
        k_ref[0, g] = kr[:, g * HEAD_DIM:(g + 1) * HEAD_DIM].astype(BF16)
    kn2 = jnp.dot((kn * kn).astype(BF16), hm, preferred_element_type=F32) * HEAD_DIM
    k2_ref[0, 0] = jnp.broadcast_to(jnp.max(kn2, axis=0, keepdims=True), (SUBLANES, LANES))

    u_ref[0] = nat[:, KV_WIDTH:]


def _in_proj(x, g1, w_nat, w_tr, qg, kg, hm, tables):
    B, S, _ = x.shape
    T = SEQ_TILE
    assert S % T == 0
    nt = S // T
    const = lambda b, i: (0, 0)
    cos, sup, sdn, cos_t, sin_t = tables
    return pl.pallas_call(
        _in_proj_kernel,
        grid=(B, nt),
        in_specs=[
            pl.BlockSpec((1, T, D_MODEL), lambda b, i: (b, i, 0)),
            pl.BlockSpec((1, D_MODEL), const),
            pl.BlockSpec((D_MODEL, NAT_WIDTH), const),
            pl.BlockSpec((TR_WIDTH, D_MODEL), const),
            pl.BlockSpec((HEAD_DIM, 1), const),
            pl.BlockSpec((1, KV_WIDTH), const),
            pl.BlockSpec((LANES, LANES), const),
            pl.BlockSpec((T, LANES), lambda b, i: (i, 0)),
            pl.BlockSpec((T, LANES), lambda b, i: (i, 0)),
            pl.BlockSpec((T, LANES), lambda b, i: (i, 0)),
            pl.BlockSpec((HEAD_DIM, T), lambda b, i: (0, i)),
            pl.BlockSpec((HEAD_DIM, T), lambda b, i: (0, i)),
        ],
        out_specs=[
            pl.BlockSpec((1, N_Q_HEADS, T // Q_SUB, HEAD_DIM, Q_SUB), lambda b, i: (b, 0, i, 0, 0)),
            pl.BlockSpec((1, N_Q_HEADS, T // Q_SUB, 1, Q_SUB), lambda b, i: (b, 0, i, 0, 0)),
            pl.BlockSpec((1, 1, N_Q_HEADS, LANES), lambda b, i: (b, i, 0, 0)),
            pl.BlockSpec((1, N_KV_HEADS, T, HEAD_DIM), lambda b, i: (b, 0, i, 0)),
            pl.BlockSpec((1, 1, SUBLANES, LANES), lambda b, i: (b, i, 0, 0)),
            pl.BlockSpec((1, N_KV_HEADS, 1, V_ROWS, T), lambda b, i: (b, 0, i, 0, 0)),
            pl.BlockSpec((1, T, POOL_WIDTH), lambda b, i: (b, i, 0)),
        ],
        out_shape=[
            jax.ShapeDtypeStruct((B, N_Q_HEADS, S // Q_SUB, HEAD_DIM, Q_SUB), BF16),
            jax.ShapeDtypeStruct((B, N_Q_HEADS, S // Q_SUB, 1, Q_SUB), F32),
            jax.ShapeDtypeStruct((B, nt, N_Q_HEADS, LANES), F32),
            jax.ShapeDtypeStruct((B, N_KV_HEADS, S, HEAD_DIM), BF16),
            jax.ShapeDtypeStruct((B, nt, SUBLANES, LANES), F32),
            jax.ShapeDtypeStruct((B, N_KV_HEADS, nt, V_ROWS, T), BF16),
            jax.ShapeDtypeStruct((B, S, POOL_WIDTH), F32),
        ],
        compiler_params=pltpu.CompilerParams(
            dimension_semantics=("parallel", "parallel"),
            vmem_limit_bytes=VMEM_LIMIT_BYTES),
        name="in_proj",
    )(x, g1, w_nat, w_tr, qg, kg, hm, cos, sup, sdn, cos_t, sin_t)


def _attn_kernel(safe_ref, qt_ref, qn_ref, k_ref, k2_ref, vt_ref, o_ref,
                 s_ref, p_ref, m_ref, al_ref, acc_ref):
    n_kt = vt_ref.shape[2]
    n_sub = qt_ref.shape[2]
    lanes = Q_PER_KV * Q_SUB
    k2_max = jnp.concatenate([k2_ref[0, 0]] * (lanes // LANES), axis=1)

    def queries(sub):
        return jnp.concatenate([qt_ref[0, h, sub] for h in range(Q_PER_KV)], axis=1)

    def finish(sub, acc):
        out_t = acc[0:HEAD_DIM] / acc[HEAD_DIM:HEAD_DIM + 1]
        out = jnp.concatenate([out_t[:, h * Q_SUB:(h + 1) * Q_SUB].T for h in range(Q_PER_KV)],
                              axis=1)
        o_ref[0, pl.ds(pl.multiple_of(sub * Q_SUB, Q_SUB), Q_SUB), :] = out.astype(o_ref.dtype)

    def shifted_by_bound():
        per_tile = KEY_TILE // STREAM_KEYS
        n_blk = n_kt * per_tile
        for sub in range(n_sub):
            qt = queries(sub)
            qn2 = jnp.concatenate([qn_ref[0, h, sub] for h in range(Q_PER_KV)], axis=1)
            bound = jnp.sqrt(qn2 * k2_max) * BOUND_MARGIN

            def scores(b):
                return jnp.dot(k_ref[0, 0, b * STREAM_KEYS:(b + 1) * STREAM_KEYS, :], qt,
                               preferred_element_type=F32)

            acc = jnp.zeros((V_ROWS, lanes), F32)
            pending = [scores(b) for b in range(min(STREAM_AHEAD, n_blk))]
            for b in range(n_blk):
                if b + STREAM_AHEAD < n_blk:
                    pending.append(scores(b + STREAM_AHEAD))
                p = jnp.exp2(pending.pop(0) - bound).astype(BF16)
                j, part = divmod(b, per_tile)
                vt = vt_ref[0, 0, j, :, part * STREAM_KEYS:(part + 1) * STREAM_KEYS]
                acc = acc + jnp.dot(vt, p, preferred_element_type=F32)
            finish(sub, acc)

    def online_max():
        def one_sub_tile(sub, carry):
            qt = queries(sub)
            m_ref[...] = jnp.full(m_ref.shape, -1e30, F32)
            acc_ref[...] = jnp.zeros(acc_ref.shape, F32)

            def scores(j):
                s_ref[j % 2] = jnp.dot(k_ref[0, 0, j * KEY_TILE:(j + 1) * KEY_TILE, :], qt,
                                       preferred_element_type=F32)

            def softmax(j):
                s = s_ref[j % 2]
                m_old = m_ref[...]
                m_new = jnp.maximum(m_old, jnp.max(s, axis=0, keepdims=True))
                al_ref[j % 2] = jnp.exp2(m_old - m_new)
                m_ref[...] = m_new
                p_ref[j % 2] = jnp.exp2(s - m_new).astype(BF16)

            def accumulate(j):
                pv = jnp.dot(vt_ref[0, 0, j], p_ref[j % 2], preferred_element_type=F32)
                acc_ref[...] = al_ref[j % 2] * acc_ref[...] + pv

            for t in range(n_kt + 2):
                if t < n_kt:
                    scores(t)
                if 1 <= t <= n_kt:
                    softmax(t - 1)
                if t >= 2:
                    accumulate(t - 2)
            finish(sub, acc_ref[...])
            return carry

        jax.lax.fori_loop(0, n_sub, one_sub_tile, 0)

    jax.lax.cond(safe_ref[pl.program_id(0), pl.program_id(1)] != 0, shifted_by_bound, online_max)


def _attention(qt, qn2, q2_tiles, k, k2_tiles, vt):
    B, _, n_qs, _, _ = qt.shape
    S = n_qs * Q_SUB
    n_kt = vt.shape[2]
    n_sub = Q_TILE // Q_SUB
    lanes = Q_PER_KV * Q_SUB

    q2_max = jnp.max(q2_tiles[:, :, :, 0], axis=1).reshape(B, N_KV_HEADS, Q_PER_KV).max(axis=-1)
    k2_max = jnp.max(k2_tiles[:, :, 0, ::HEAD_DIM], axis=1)
    safe = (jnp.sqrt(q2_max * k2_max) * BOUND_MARGIN <= MAX_BOUND_SHIFT).astype(jnp.int32)
    k2_lanes = jnp.broadcast_to(k2_max[:, :, None, None], (B, N_KV_HEADS, 1, LANES))

    grid_spec = pltpu.PrefetchScalarGridSpec(
        num_scalar_prefetch=1,
        grid=(B, N_KV_HEADS, S // Q_TILE),
        in_specs=[
            pl.BlockSpec((1, Q_PER_KV, n_sub, HEAD_DIM, Q_SUB),
                         lambda b, g, i, safe: (b, g, i, 0, 0)),
            pl.BlockSpec((1, Q_PER_KV, n_sub, 1, Q_SUB), lambda b, g, i, safe: (b, g, i, 0, 0)),
            pl.BlockSpec((1, 1, S, HEAD_DIM), lambda b, g, i, safe: (b, g, 0, 0)),
            pl.BlockSpec((1, 1, 1, LANES), lambda b, g, i, safe: (b, g, 0, 0)),
            pl.BlockSpec((1, 1, n_kt, V_ROWS, KEY_TILE), lambda b, g, i, safe: (b, g, 0, 0, 0)),
        ],
        out_specs=pl.BlockSpec((1, Q_TILE, Q_PER_KV * HEAD_DIM), lambda b, g, i, safe: (b, i, g)),
        scratch_shapes=[
            pltpu.VMEM((2, KEY_TILE, lanes), F32),
            pltpu.VMEM((2, KEY_TILE, lanes), BF16),
            pltpu.VMEM((1, lanes), F32),
            pltpu.VMEM((2, 1, lanes), F32),
            pltpu.VMEM((V_ROWS, lanes), F32),
        ],
    )
    return pl.pallas_call(
        _attn_kernel,
        grid_spec=grid_spec,
        out_shape=jax.ShapeDtypeStruct((B, S, ATTN_WIDTH), BF16),
        compiler_params=pltpu.CompilerParams(
            dimension_semantics=("parallel", "parallel", "arbitrary"),
            vmem_limit_bytes=VMEM_LIMIT_BYTES),
        name="attention",
    )(safe, qt, qn2, k, k2_lanes, vt)


def _mix_mlp_kernel(x_ref, a_ref, u_ref, up_ref, un_ref, wp_ref, ps_ref, wo_ref, g2_ref,
                    wu_ref, wd_ref, y_ref, ue_ref):
    i = pl.program_id(1)
    nt = pl.num_programs(1)
    T = x_ref.shape[1]
    S = T * nt

    u = u_ref[0]
    ue_ref[POOL_HALO:POOL_HALO + T, :] = u
    ue_ref[0:POOL_HALO, :] = jnp.where(i > 0, up_ref[0], 0.0)
    ue_ref[POOL_HALO + T:, :] = jnp.where(i < nt - 1, un_ref[0], 0.0)

    t = i * T + jax.lax.broadcasted_iota(jnp.int32, (T, 1), 0)
    mix = jnp.dot(a_ref[0], wo_ref[0:ATTN_WIDTH, :], preferred_element_type=F32)
    for g, win in enumerate(POOL_WINDOWS):
        sl = slice(g * POOL_GROUP_DIM, (g + 1) * POOL_GROUP_DIM)
        half = win // 2
        wsum = ue_ref[POOL_HALO - half:POOL_HALO - half + T, sl]
        for j in range(1 - half, half):
            wsum = wsum + ue_ref[POOL_HALO + j:POOL_HALO + j + T, sl]
        cnt = (jnp.minimum(t + half, S) - jnp.maximum(t - half, 0)).astype(F32)
        d = wsum / cnt - u[:, sl]
        z = jnp.dot(d.astype(BF16), wp_ref[g], preferred_element_type=F32) * ps_ref[:, sl]
        mix = mix + jnp.dot(z.astype(BF16), wo_ref[ATTN_WIDTH + g * POOL_GROUP_DIM:
                                                   ATTN_WIDTH + (g + 1) * POOL_GROUP_DIM, :],
                            preferred_element_type=F32)

    x1 = x_ref[0] + mix
    ms = jnp.mean(x1 * x1, axis=-1, keepdims=True)
    h2 = (x1 * jax.lax.rsqrt(ms + EPS) * g2_ref[...]).astype(BF16)
    acc = x1
    for c in range(D_FF // FF_CHUNK):
        sl = slice(c * FF_CHUNK, (c + 1) * FF_CHUNK)
        f = jnp.dot(h2, wu_ref[:, sl], preferred_element_type=F32)
        f = jnp.square(jnp.maximum(f, 0.0)).astype(BF16)
        acc = acc + jnp.dot(f, wd_ref[sl, :], preferred_element_type=F32)
    y_ref[0] = acc


def _mix_mlp(x, a, u, w_pool, pool_scale, w_out, g2, w_up, w_down):
    B, S, _ = x.shape
    T = SEQ_TILE
    nt = S // T
    hb = T // POOL_HALO
    n_hb = S // POOL_HALO
    const2 = lambda b, i: (0, 0)
    resident = functools.partial(pl.BlockSpec, pipeline_mode=pl.Buffered(1))
    return pl.pallas_call(
        _mix_mlp_kernel,
        grid=(B, nt),
        in_specs=[
            pl.BlockSpec((1, T, D_MODEL), lambda b, i: (b, i, 0)),
            pl.BlockSpec((1, T, ATTN_WIDTH), lambda b, i: (b, i, 0)),
            pl.BlockSpec((1, T, POOL_WIDTH), lambda b, i: (b, i, 0)),
            pl.BlockSpec((1, POOL_HALO, POOL_WIDTH),
                         lambda b, i: (b, jnp.maximum(i * hb - 1, 0), 0)),
            pl.BlockSpec((1, POOL_HALO, POOL_WIDTH),
                         lambda b, i: (b, jnp.minimum((i + 1) * hb, n_hb - 1), 0)),
            resident((N_POOL_GROUPS, POOL_GROUP_DIM, POOL_GROUP_DIM), lambda b, i: (0, 0, 0)),
            pl.BlockSpec((1, POOL_WIDTH), const2),
            resident((D_MODEL, D_MODEL), const2),
            pl.BlockSpec((1, D_MODEL), const2),
            resident((D_MODEL, D_FF), const2),
            resident((D_FF, D_MODEL), const2),
        ],
        out_specs=pl.BlockSpec((1, T, D_MODEL), lambda b, i: (b, i, 0)),
        out_shape=jax.ShapeDtypeStruct((B, S, D_MODEL), F32),
        scratch_shapes=[pltpu.VMEM((T + 2 * POOL_HALO, POOL_WIDTH), F32)],
        compiler_params=pltpu.CompilerParams(
            dimension_semantics=("parallel", "parallel"),
            vmem_limit_bytes=VMEM_LIMIT_BYTES),
        name="mix_mlp",
    )(x, a, u, u, u, w_pool, pool_scale, w_out, g2, w_up, w_down)


def _layer(x, g1, w_nat, w_tr, qg, kg, w_pool, pool_scale, w_out, g2, w_up, w_down, hm, tables):
    qt, qn2, q2_tiles, k, k2_tiles, vt, u = _in_proj(x, g1, w_nat, w_tr, qg, kg, hm, tables)
    a = _attention(qt, qn2, q2_tiles, k, k2_tiles, vt)
    return _mix_mlp(x, a, u, w_pool, pool_scale, w_out, g2, w_up, w_down)


def _trunk(x, layers, hm):
    tables = _rope_tables(x.shape[1])
    for params in layers:
        x = _layer(x, *params, hm, tables)
    return x


def kernel(x_prompt, x_sample, norm1_g, w_in, q_norm_g, k_norm_g, w_pool, pool_scale, w_out,
           norm2_g, w_up, w_down):
    depth = w_in.shape[0]
    q_end, k_end, v_end = ATTN_WIDTH, ATTN_WIDTH + KV_WIDTH, ATTN_WIDTH + 2 * KV_WIDTH
    layers = []
    for l in range(depth):
        w = w_in[l].astype(BF16)
        layers.append((
            norm1_g[l][None, :],
            jnp.concatenate([w[:, q_end:k_end], w[:, v_end:]], axis=1),
            jnp.concatenate([w[:, :q_end], w[:, k_end:v_end]], axis=1).T,
            q_norm_g[l][:, None],
            jnp.tile(k_norm_g[l], N_KV_HEADS)[None, :],
            w_pool[l].astype(BF16),
            pool_scale[l][None, :],
            w_out[l].astype(BF16),
            norm2_g[l][None, :],
            w_up[l].astype(BF16),
            w_down[l].astype(BF16),
        ))
    hm = _head_mean_matrix()
    return (_trunk(x_prompt, layers, hm), _trunk(x_sample, layers, hm))
```

```python
import functools
import math

import jax
import jax.numpy as jnp
from jax.experimental import pallas as pl
from jax.experimental.pallas import tpu as pltpu

D_MODEL = 1024
GRID_W = 64
HEAD_DIM = 64
N_Q_HEADS = 8
N_KV_HEADS = 2
Q_PER_KV = N_Q_HEADS // N_KV_HEADS
ATTN_WIDTH = N_Q_HEADS * HEAD_DIM
KV_WIDTH = N_KV_HEADS * HEAD_DIM
POOL_WINDOWS = (2, 4, 8, 16)
N_POOL_GROUPS = len(POOL_WINDOWS)
POOL_WIDTH = D_MODEL - ATTN_WIDTH
POOL_GROUP_DIM = POOL_WIDTH // N_POOL_GROUPS
IN_WIDTH = ATTN_WIDTH + 2 * KV_WIDTH + POOL_WIDTH
D_FF = 4 * D_MODEL
ROPE_THETA = 10000.0
EPS = 1e-6
N_FREQ = HEAD_DIM // 4

LANES = 128
SUBLANES = 8
BF16_ROWS = 16
POOL_HALO = 8
assert all(w & (w - 1) == 0 and w // 2 <= POOL_HALO for w in POOL_WINDOWS)
VMEM_LIMIT_BYTES = 56 * 1024 * 1024

PROJ_TILE = 1024
SEQ_TILE = 512
KEY_TILE = 512
Q_SUB = 128
Q_TILE = 512
FF_CHUNK = 1024
V_ROWS = HEAD_DIM + BF16_ROWS
MAX_BOUND_SHIFT = 30.0
BOUND_MARGIN = 1.02
STREAM_KEYS = 256
STREAM_AHEAD = 2

F32 = jnp.float32
BF16 = jnp.bfloat16

NAT_WIDTH = KV_WIDTH + POOL_WIDTH
TR_WIDTH = ATTN_WIDTH + KV_WIDTH


def _rope_angles(seq_len):
    rows = seq_len // GRID_W
    row_ids = jnp.broadcast_to(jnp.arange(rows, dtype=F32)[:, None], (rows, GRID_W)).reshape(-1)
    col_ids = jnp.broadcast_to(jnp.arange(GRID_W, dtype=F32)[None, :], (rows, GRID_W)).reshape(-1)
    inv_freq = ROPE_THETA ** (-jnp.arange(N_FREQ, dtype=F32) / N_FREQ)
    ang = jnp.stack([row_ids[:, None] * inv_freq, col_ids[:, None] * inv_freq], axis=1)
    return jnp.cos(ang), jnp.sin(ang)


def _rope_tables(seq_len):
    cos, sin = _rope_angles(seq_len)
    zero = jnp.zeros_like(sin)
    cos_h = jnp.stack([cos, cos], axis=2).reshape(seq_len, HEAD_DIM)
    sin_up = jnp.stack([-sin, zero], axis=2).reshape(seq_len, HEAD_DIM)
    sin_dn = jnp.stack([zero, sin], axis=2).reshape(seq_len, HEAD_DIM)
    sin_sg = jnp.stack([-sin, sin], axis=2).reshape(seq_len, HEAD_DIM)
    rep = LANES // HEAD_DIM
    return (jnp.tile(cos_h, (1, rep)), jnp.tile(sin_up, (1, rep)), jnp.tile(sin_dn, (1, rep)),
            cos_h.T, sin_sg.T)


def _head_mean_matrix():
    idx = jnp.arange(LANES) // HEAD_DIM
    return jnp.where(idx[:, None] == idx[None, :], 1.0 / HEAD_DIM, 0.0).astype(BF16)


def _in_proj_kernel(x_ref, g1_ref, w_nat_ref, w_tr_ref, qg_ref, kg_ref, hm_ref,
                    cos_ref, sup_ref, sdn_ref, cos_t_ref, sin_t_ref,
                    qt_ref, qn_ref, q2_ref, k_ref, k2_ref, vt_ref, u_ref):
    x = x_ref[0]
    ms = jnp.mean(x * x, axis=-1, keepdims=True)
    h = (x * jax.lax.rsqrt(ms + EPS) * g1_ref[...]).astype(BF16)
    nat = jnp.dot(h, w_nat_ref[...], preferred_element_type=F32)
    tr = jax.lax.dot_general(w_tr_ref[...], h, (((1,), (1,)), ((), ())),
                             preferred_element_type=F32)

    q_gain = qg_ref[...] * (HEAD_DIM ** -0.5 * math.log2(math.e))
    cos_t, sin_t = cos_t_ref[...], sin_t_ref[...]
    q2_rows = []
    for hd in range(N_Q_HEADS):
        t = tr[hd * HEAD_DIM:(hd + 1) * HEAD_DIM, :]
        tn = t * jax.lax.rsqrt(jnp.mean(t * t, axis=0, keepdims=True) + EPS) * q_gain
        partner = jnp.concatenate([tn[N_FREQ:2 * N_FREQ], tn[0:N_FREQ],
                                   tn[3 * N_FREQ:4 * N_FREQ], tn[2 * N_FREQ:3 * N_FREQ]], axis=0)
        qr = (tn * cos_t + partner * sin_t).astype(BF16)
        qn2 = jnp.sum(tn * tn, axis=0, keepdims=True)
        for c in range(tr.shape[1] // Q_SUB):
            qt_ref[0, hd, c] = qr[:, c * Q_SUB:(c + 1) * Q_SUB]
            qn_ref[0, hd, c] = qn2[:, c * Q_SUB:(c + 1) * Q_SUB]
        q2_rows.append(jnp.broadcast_to(jnp.max(qn2, axis=1, keepdims=True), (1, LANES)))
    q2_ref[0, 0] = jnp.concatenate(q2_rows, axis=0)

    row = jax.lax.broadcasted_iota(jnp.int32, (BF16_ROWS, KEY_TILE), 0)
    ones_rows = jnp.where(row == 0, 1.0, 0.0).astype(BF16)
    for g in range(N_KV_HEADS):
        lo = ATTN_WIDTH + g * HEAD_DIM
        vg = tr[lo:lo + HEAD_DIM, :].astype(BF16)
        for c in range(tr.shape[1] // KEY_TILE):
            vt_ref[0, g, c, 0:HEAD_DIM, :] = vg[:, c * KEY_TILE:(c + 1) * KEY_TILE]
            vt_ref[0, g, c, HEAD_DIM:, :] = ones_rows

    kk = nat[:, 0:KV_WIDTH]
    k2 = kk * kk
    hi = k2.astype(BF16)
    lo2 = (k2 - hi.astype(F32)).astype(BF16)
    hm = hm_ref[...]
    ms_k = (jnp.dot(hi, hm, preferred_element_type=F32)
            + jnp.dot(lo2, hm, preferred_element_type=F32))
    kn = kk * jax.lax.rsqrt(ms_k + EPS) * kg_ref[...]
    kr = (kn * cos_ref[...] + pltpu.roll(kn, LANES - N_FREQ, 1) * sup_ref[...]
          + pltpu.roll(kn, N_FREQ, 1) * sdn_ref[...])
    for g in range(N_KV_HEADS):
        k_ref[0, g] = kr[:, g * HEAD_DIM:(g + 1) * HEAD_DIM].astype(BF16)
    kn2 = jnp.dot((kn * kn).astype(BF16), hm, preferred_element_type=F32) * HEAD_DIM
    k2_ref[0, 0] = jnp.broadcast_to(jnp.max(kn2, axis=0, keepdims=True), (SUBLANES, LANES))

    u_ref[0] = nat[:, KV_WIDTH:]


def _in_proj(x, g1, w_nat, w_tr, qg, kg, hm, tables):
    B, S, _ = x.shape
    T = PROJ_TILE
    assert S % T == 0 and T % KEY_TILE == 0 and T % Q_SUB == 0
    nt = S // T
    const = lambda b, i: (0, 0)
    cos, sup, sdn, cos_t, sin_t = tables
    return pl.pallas_call(
        _in_proj_kernel,
        grid=(B, nt),
        in_specs=[
            pl.BlockSpec((1, T, D_MODEL), lambda b, i: (b, i, 0)),
            pl.BlockSpec((1, D_MODEL), const),
            pl.BlockSpec((D_MODEL, NAT_WIDTH), const),
            pl.BlockSpec((TR_WIDTH, D_MODEL), const),
            pl.BlockSpec((HEAD_DIM, 1), const),
            pl.BlockSpec((1, KV_WIDTH), const),
            pl.BlockSpec((LANES, LANES), const),
            pl.BlockSpec((T, LANES), lambda b, i: (i, 0)),
            pl.BlockSpec((T, LANES), lambda b, i: (i, 0)),
            pl.BlockSpec((T, LANES), lambda b, i: (i, 0)),
            pl.BlockSpec((HEAD_DIM, T), lambda b, i: (0, i)),
            pl.BlockSpec((HEAD_DIM, T), lambda b, i: (0, i)),
        ],
        out_specs=[
            pl.BlockSpec((1, N_Q_HEADS, T // Q_SUB, HEAD_DIM, Q_SUB), lambda b, i: (b, 0, i, 0, 0)),
            pl.BlockSpec((1, N_Q_HEADS, T // Q_SUB, 1, Q_SUB), lambda b, i: (b, 0, i, 0, 0)),
            pl.BlockSpec((1, 1, N_Q_HEADS, LANES), lambda b, i: (b, i, 0, 0)),
            pl.BlockSpec((1, N_KV_HEADS, T, HEAD_DIM), lambda b, i: (b, 0, i, 0)),
            pl.BlockSpec((1, 1, SUBLANES, LANES), lambda b, i: (b, i, 0, 0)),
            pl.BlockSpec((1, N_KV_HEADS, T // KEY_TILE, V_ROWS, KEY_TILE),
                         lambda b, i: (b, 0, i, 0, 0)),
            pl.BlockSpec((1, T, POOL_WIDTH), lambda b, i: (b, i, 0)),
        ],
        out_shape=[
            jax.ShapeDtypeStruct((B, N_Q_HEADS, S // Q_SUB, HEAD_DIM, Q_SUB), BF16),
            jax.ShapeDtypeStruct((B, N_Q_HEADS, S // Q_SUB, 1, Q_SUB), F32),
            jax.ShapeDtypeStruct((B, nt, N_Q_HEADS, LANES), F32),
            jax.ShapeDtypeStruct((B, N_KV_HEADS, S, HEAD_DIM), BF16),
            jax.ShapeDtypeStruct((B, nt, SUBLANES, LANES), F32),
            jax.ShapeDtypeStruct((B, N_KV_HEADS, S // KEY_TILE, V_ROWS, KEY_TILE), BF16),
            jax.ShapeDtypeStruct((B, S, POOL_WIDTH), F32),
        ],
        compiler_params=pltpu.CompilerParams(
            dimension_semantics=("parallel", "parallel"),
            vmem_limit_bytes=VMEM_LIMIT_BYTES),
        name="in_proj",
    )(x, g1, w_nat, w_tr, qg, kg, hm, cos, sup, sdn, cos_t, sin_t)


def _attn_kernel(safe_ref, qt_ref, qn_ref, k_ref, k2_ref, vt_ref, o_ref,
                 s_ref, p_ref, m_ref, al_ref, acc_ref):
    n_kt = vt_ref.shape[2]
    n_sub = qt_ref.shape[2]
    lanes = Q_PER_KV * Q_SUB
    k2_max = jnp.concatenate([k2_ref[0, 0]] * (lanes // LANES), axis=1)

    def queries(sub):
        return jnp.concatenate([qt_ref[0, h, sub] for h in range(Q_PER_KV)], axis=1)

    def finish(sub, acc):
        out_t = acc[0:HEAD_DIM] / acc[HEAD_DIM:HEAD_DIM + 1]
        out = jnp.concatenate([out_t[:, h * Q_SUB:(h + 1) * Q_SUB].T for h in range(Q_PER_KV)],
                              axis=1)
        o_ref[0, pl.ds(pl.multiple_of(sub * Q_SUB, Q_SUB), Q_SUB), :] = out.astype(o_ref.dtype)

    def shifted_by_bound():
        per_tile = KEY_TILE // STREAM_KEYS
        n_blk = n_kt * per_tile
        for sub in range(n_sub):
            qt = queries(sub)
            qn2 = jnp.concatenate([qn_ref[0, h, sub] for h in range(Q_PER_KV)], axis=1)
            bound = jnp.sqrt(qn2 * k2_max) * BOUND_MARGIN

            def scores(b):
                return jnp.dot(k_ref[0, 0, b * STREAM_KEYS:(b + 1) * STREAM_KEYS, :], qt,
                               preferred_element_type=F32)

            acc = jnp.zeros((V_ROWS, lanes), F32)
            pending = [scores(b) for b in range(min(STREAM_AHEAD, n_blk))]
            for b in range(n_blk):
                if b + STREAM_AHEAD < n_blk:
                    pending.append(scores(b + STREAM_AHEAD))
                p = jnp.exp2(pending.pop(0) - bound).astype(BF16)
                j, part = divmod(b, per_tile)
                vt = vt_ref[0, 0, j, :, part * STREAM_KEYS:(part + 1) * STREAM_KEYS]
                acc = acc + jnp.dot(vt, p, preferred_element_type=F32)
            finish(sub, acc)

    def online_max():
        def one_sub_tile(sub, carry):
            qt = queries(sub)
            m_ref[...] = jnp.full(m_ref.shape, -1e30, F32)
            acc_ref[...] = jnp.zeros(acc_ref.shape, F32)

            def scores(j):
                s_ref[j % 2] = jnp.dot(k_ref[0, 0, j * KEY_TILE:(j + 1) * KEY_TILE, :], qt,
                                       preferred_element_type=F32)

            def softmax(j):
                s = s_ref[j % 2]
                m_old = m_ref[...]
                m_new = jnp.maximum(m_old, jnp.max(s, axis=0, keepdims=True))
                al_ref[j % 2] = jnp.exp2(m_old - m_new)
                m_ref[...] = m_new
                p_ref[j % 2] = jnp.exp2(s - m_new).astype(BF16)

            def accumulate(j):
                pv = jnp.dot(vt_ref[0, 0, j], p_ref[j % 2], preferred_element_type=F32)
                acc_ref[...] = al_ref[j % 2] * acc_ref[...] + pv

            for t in range(n_kt + 2):
                if t < n_kt:
                    scores(t)
                if 1 <= t <= n_kt:
                    softmax(t - 1)
                if t >= 2:
                    accumulate(t - 2)
            finish(sub, acc_ref[...])
            return carry

        jax.lax.fori_loop(0, n_sub, one_sub_tile, 0)

    jax.lax.cond(safe_ref[pl.program_id(0), pl.program_id(1)] != 0, shifted_by_bound, online_max)


def _attention(qt, qn2, q2_tiles, k, k2_tiles, vt):
    B, _, n_qs, _, _ = qt.shape
    S = n_qs * Q_SUB
    n_kt = vt.shape[2]
    n_sub = Q_TILE // Q_SUB
    lanes = Q_PER_KV * Q_SUB

    q2_max = jnp.max(q2_tiles[:, :, :, 0], axis=1).reshape(B, N_KV_HEADS, Q_PER_KV).max(axis=-1)
    k2_max = jnp.max(k2_tiles[:, :, 0, ::HEAD_DIM], axis=1)
    safe = (jnp.sqrt(q2_max * k2_max) * BOUND_MARGIN <= MAX_BOUND_SHIFT).astype(jnp.int32)
    k2_lanes = jnp.broadcast_to(k2_max[:, :, None, None], (B, N_KV_HEADS, 1, LANES))

    grid_spec = pltpu.PrefetchScalarGridSpec(
        num_scalar_prefetch=1,
        grid=(B, N_KV_HEADS, S // Q_TILE),
        in_specs=[
            pl.BlockSpec((1, Q_PER_KV, n_sub, HEAD_DIM, Q_SUB),
                         lambda b, g, i, safe: (b, g, i, 0, 0)),
            pl.BlockSpec((1, Q_PER_KV, n_sub, 1, Q_SUB), lambda b, g, i, safe: (b, g, i, 0, 0)),
            pl.BlockSpec((1, 1, S, HEAD_DIM), lambda b, g, i, safe: (b, g, 0, 0)),
            pl.BlockSpec((1, 1, 1, LANES), lambda b, g, i, safe: (b, g, 0, 0)),
            pl.BlockSpec((1, 1, n_kt, V_ROWS, KEY_TILE), lambda b, g, i, safe: (b, g, 0, 0, 0)),
        ],
        out_specs=pl.BlockSpec((1, Q_TILE, Q_PER_KV * HEAD_DIM), lambda b, g, i, safe: (b, i, g)),
        scratch_shapes=[
            pltpu.VMEM((2, KEY_TILE, lanes), F32),
            pltpu.VMEM((2, KEY_TILE, lanes), BF16),
            pltpu.VMEM((1, lanes), F32),
            pltpu.VMEM((2, 1, lanes), F32),
            pltpu.VMEM((V_ROWS, lanes), F32),
        ],
    )
    return pl.pallas_call(
        _attn_kernel,
        grid_spec=grid_spec,
        out_shape=jax.ShapeDtypeStruct((B, S, ATTN_WIDTH), BF16),
        compiler_params=pltpu.CompilerParams(
            dimension_semantics=("parallel", "parallel", "arbitrary"),
            vmem_limit_bytes=VMEM_LIMIT_BYTES),
        name="attention",
    )(safe, qt, qn2, k, k2_lanes, vt)


def _mix_mlp_kernel(x_ref, a_ref, u_ref, up_ref, un_ref, wp_ref, ps_ref, wo_ref, g2_ref,
                    wu_ref, wd_ref, y_ref):
    i = pl.program_id(1)
    nt = pl.num_programs(1)
    T = x_ref.shape[1]
    S = T * nt
    n_ext = T + 2 * POOL_HALO

    u = u_ref[0]
    ext = jnp.concatenate([jnp.where(i > 0, up_ref[0], 0.0), u,
                           jnp.where(i < nt - 1, un_ref[0], 0.0)], axis=0)

    def rows_from(v, k):
        return pltpu.roll(v, (-k) % n_ext, 0) if k % n_ext else v

    row = jax.lax.broadcasted_iota(jnp.int32, (POOL_HALO, 1), 0)
    t_first = i * T + row
    t_last = i * T + (T - POOL_HALO) + row

    mix = jnp.dot(a_ref[0], wo_ref[0:ATTN_WIDTH, :], preferred_element_type=F32)
    for g, win in enumerate(POOL_WINDOWS):
        sl = slice(g * POOL_GROUP_DIM, (g + 1) * POOL_GROUP_DIM)
        half = win // 2
        run, length = ext[:, sl], 1
        while length < win:
            run = run + rows_from(run, -length)
            length *= 2
        wsum = rows_from(run, half - 1)[POOL_HALO:POOL_HALO + T]

        def clipped(tok):
            return (jnp.minimum(tok + half, S) - jnp.maximum(tok - half, 0)).astype(F32)

        ug = u[:, sl]
        inner = slice(POOL_HALO, T - POOL_HALO)
        d = jnp.concatenate([
            wsum[:POOL_HALO] / clipped(t_first) - ug[:POOL_HALO],
            wsum[inner] * (1.0 / win) - ug[inner],
            wsum[T - POOL_HALO:] / clipped(t_last) - ug[T - POOL_HALO:]], axis=0)
        z = jnp.dot(d.astype(BF16), wp_ref[g], preferred_element_type=F32) * ps_ref[:, sl]
        mix = mix + jnp.dot(z.astype(BF16), wo_ref[ATTN_WIDTH + g * POOL_GROUP_DIM:
                                                   ATTN_WIDTH + (g + 1) * POOL_GROUP_DIM, :],
                            preferred_element_type=F32)

    x1 = x_ref[0] + mix
    ms = jnp.mean(x1 * x1, axis=-1, keepdims=True)
    h2 = (x1 * jax.lax.rsqrt(ms + EPS) * g2_ref[...]).astype(BF16)
    acc = x1
    for c in range(D_FF // FF_CHUNK):
        sl = slice(c * FF_CHUNK, (c + 1) * FF_CHUNK)
        f = jnp.dot(h2, wu_ref[:, sl], preferred_element_type=F32)
        f = jnp.square(jnp.maximum(f, 0.0)).astype(BF16)
        acc = acc + jnp.dot(f, wd_ref[sl, :], preferred_element_type=F32)
    y_ref[0] = acc


def _mix_mlp(x, a, u, w_pool, pool_scale, w_out, g2, w_up, w_down):
    B, S, _ = x.shape
    T = SEQ_TILE
    nt = S // T
    hb = T // POOL_HALO
    n_hb = S // POOL_HALO
    const2 = lambda b, i: (0, 0)
    resident = functools.partial(pl.BlockSpec, pipeline_mode=pl.Buffered(1))
    return pl.pallas_call(
        _mix_mlp_kernel,
        grid=(B, nt),
        in_specs=[
            pl.BlockSpec((1, T, D_MODEL), lambda b, i: (b, i, 0)),
            pl.BlockSpec((1, T, ATTN_WIDTH), lambda b, i: (b, i, 0)),
            pl.BlockSpec((1, T, POOL_WIDTH), lambda b, i: (b, i, 0)),
            pl.BlockSpec((1, POOL_HALO, POOL_WIDTH),
                         lambda b, i: (b, jnp.maximum(i * hb - 1, 0), 0)),
            pl.BlockSpec((1, POOL_HALO, POOL_WIDTH),
                         lambda b, i: (b, jnp.minimum((i + 1) * hb, n_hb - 1), 0)),
            resident((N_POOL_GROUPS, POOL_GROUP_DIM, POOL_GROUP_DIM), lambda b, i: (0, 0, 0)),
            pl.BlockSpec((1, POOL_WIDTH), const2),
            resident((D_MODEL, D_MODEL), const2),
            pl.BlockSpec((1, D_MODEL), const2),
            resident((D_MODEL, D_FF), const2),
            resident((D_FF, D_MODEL), const2),
        ],
        out_specs=pl.BlockSpec((1, T, D_MODEL), lambda b, i: (b, i, 0)),
        out_shape=jax.ShapeDtypeStruct((B, S, D_MODEL), F32),
        compiler_params=pltpu.CompilerParams(
            dimension_semantics=("parallel", "parallel"),
            vmem_limit_bytes=VMEM_LIMIT_BYTES),
        name="mix_mlp",
    )(x, a, u, u, u, w_pool, pool_scale, w_out, g2, w_up, w_down)


def _layer(x, g1, w_nat, w_tr, qg, kg, w_pool, pool_scale, w_out, g2, w_up, w_down, hm, tables):
    qt, qn2, q2_tiles, k, k2_tiles, vt, u = _in_proj(x, g1, w_nat, w_tr, qg, kg, hm, tables)
    a = _attention(qt, qn2, q2_tiles, k, k2_tiles, vt)
    return _mix_mlp(x, a, u, w_pool, pool_scale, w_out, g2, w_up, w_down)


def _trunk(x, layers, hm):
    tables = _rope_tables(x.shape[1])
    for params in layers:
        x = _layer(x, *params, hm, tables)
    return x


def kernel(x_prompt, x_sample, norm1_g, w_in, q_norm_g, k_norm_g, w_pool, pool_scale, w_out,
           norm2_g, w_up, w_down):
    depth = w_in.shape[0]
    q_end, k_end, v_end = ATTN_WIDTH, ATTN_WIDTH + KV_WIDTH, ATTN_WIDTH + 2 * KV_WIDTH
    layers = []
    for l in range(depth):
        w = w_in[l].astype(BF16)
        layers.append((
            norm1_g[l][None, :],
            jnp.concatenate([w[:, q_end:k_end], w[:, v_end:]], axis=1),
            jnp.concatenate([w[:, :q_end], w[:, k_end:v_end]], axis=1).T,
            q_norm_g[l][:, None],
            jnp.tile(k_norm_g[l], N_KV_HEADS)[None, :],
            w_pool[l].astype(BF16),
            pool_scale[l][None, :],
            w_out[l].astype(BF16),
            norm2_g[l][None, :],
            w_up[l].astype(BF16),
            w_down[l].astype(BF16),
        ))
    hm = _head_mean_matrix()
    return (_trunk(x_prompt, layers, hm), _trunk(x_sample, layers, hm))
```

```python
import functools
import math

import jax
import jax.numpy as jnp
from jax.experimental import pallas as pl
from jax.experimental.pallas import tpu as pltpu

D_MODEL = 1024
GRID_W = 64
HEAD_DIM = 64
N_Q_HEADS = 8
N_KV_HEADS = 2
Q_PER_KV = N_Q_HEADS // N_KV_HEADS
ATTN_WIDTH = N_Q_HEADS * HEAD_DIM
KV_WIDTH = N_KV_HEADS * HEAD_DIM
POOL_WINDOWS = (2, 4, 8, 16)
N_POOL_GROUPS = len(POOL_WINDOWS)
POOL_WIDTH = D_MODEL - ATTN_WIDTH
POOL_GROUP_DIM = POOL_WIDTH // N_POOL_GROUPS
D_FF = 4 * D_MODEL
ROPE_THETA = 10000.0
EPS = 1e-6
N_FREQ = HEAD_DIM // 4

LANES = 128
SUBLANES = 8
BF16_ROWS = 16
MXU_DIM = 256
POOL_HALO = 8
assert all(w & (w - 1) == 0 and w // 2 <= POOL_HALO for w in POOL_WINDOWS)
POOL_PAIR = MXU_DIM // POOL_GROUP_DIM
assert N_POOL_GROUPS % POOL_PAIR == 0
VMEM_LIMIT_BYTES = 56 * 1024 * 1024

PROJ_TILE = 1024
SEQ_TILE = 512
KEY_TILE = 512
Q_SUB = 128
UNROLLED_PASS_TILES = 64
FF_CHUNK = 1024
V_ROWS = HEAD_DIM + BF16_ROWS
MAX_BOUND_SHIFT = 30.0
BOUND_MARGIN = 1.02
STREAM_KEYS = 256
STREAM_AHEAD = 2

F32 = jnp.float32
BF16 = jnp.bfloat16

TR_WIDTH = ATTN_WIDTH + 2 * KV_WIDTH


def _rope_tables(seq_len):
    rows = seq_len // GRID_W
    row_ids = jnp.broadcast_to(jnp.arange(rows, dtype=F32)[:, None], (rows, GRID_W)).reshape(-1)
    col_ids = jnp.broadcast_to(jnp.arange(GRID_W, dtype=F32)[None, :], (rows, GRID_W)).reshape(-1)
    inv_freq = ROPE_THETA ** (-jnp.arange(N_FREQ, dtype=F32) / N_FREQ)
    ang = jnp.stack([row_ids[:, None] * inv_freq, col_ids[:, None] * inv_freq], axis=1)
    cos, sin = jnp.cos(ang), jnp.sin(ang)
    cos_h = jnp.stack([cos, cos], axis=2).reshape(seq_len, HEAD_DIM)
    sin_sg = jnp.stack([-sin, sin], axis=2).reshape(seq_len, HEAD_DIM)
    return cos_h.T, sin_sg.T


def _in_proj_kernel(x_ref, g1_ref, w_u_ref, w_tr_ref, qg_ref, kg_ref, cos_t_ref, sin_t_ref,
                    qt_ref, qn_ref, q2_ref, k_ref, k2_ref, vt_ref, u_ref):
    x = x_ref[0]
    ms = jnp.mean(x * x, axis=-1, keepdims=True)
    h = (x * jax.lax.rsqrt(ms + EPS) * g1_ref[...]).astype(BF16)
    u_ref[0] = jnp.dot(h, w_u_ref[...], preferred_element_type=F32)
    tr = jax.lax.dot_general(w_tr_ref[...], h, (((1,), (1,)), ((), ())),
                             preferred_element_type=F32)
    n_tok = tr.shape[1]
    cos_t, sin_t = cos_t_ref[...], sin_t_ref[...]

    def norm_rope(t, gain):
        tn = t * jax.lax.rsqrt(jnp.mean(t * t, axis=0, keepdims=True) + EPS) * gain
        partner = jnp.concatenate([tn[N_FREQ:2 * N_FREQ], tn[0:N_FREQ],
                                   tn[3 * N_FREQ:4 * N_FREQ], tn[2 * N_FREQ:3 * N_FREQ]], axis=0)
        return tn * cos_t + partner * sin_t, jnp.sum(tn * tn, axis=0, keepdims=True)

    def lane_max_row(v):
        return jnp.broadcast_to(jnp.max(v, axis=1, keepdims=True), (1, LANES))

    q_gain = qg_ref[...] * (HEAD_DIM ** -0.5 * math.log2(math.e))
    q2_rows = []
    for hd in range(N_Q_HEADS):
        qr, qn2 = norm_rope(tr[hd * HEAD_DIM:(hd + 1) * HEAD_DIM, :], q_gain)
        qr = qr.astype(BF16)
        for c in range(n_tok // Q_SUB):
            qt_ref[0, hd, c] = qr[:, c * Q_SUB:(c + 1) * Q_SUB]
            qn_ref[0, hd, c] = qn2[:, c * Q_SUB:(c + 1) * Q_SUB]
        q2_rows.append(lane_max_row(qn2))
    q2_ref[0, 0] = jnp.concatenate(q2_rows, axis=0)

    k_gain = kg_ref[...]
    kr_heads, k2_rows = [], []
    for g in range(N_KV_HEADS):
        lo = ATTN_WIDTH + g * HEAD_DIM
        kr, kn2 = norm_rope(tr[lo:lo + HEAD_DIM, :], k_gain)
        kr_heads.append(kr)
        k2_rows.append(lane_max_row(kn2))
    k_tok = jnp.concatenate(kr_heads, axis=0).T
    for g in range(N_KV_HEADS):
        k_ref[0, g] = k_tok[:, g * HEAD_DIM:(g + 1) * HEAD_DIM].astype(BF16)
    k2_rows.append(jnp.zeros((SUBLANES - N_KV_HEADS, LANES), F32))
    k2_ref[0, 0] = jnp.concatenate(k2_rows, axis=0)

    row = jax.lax.broadcasted_iota(jnp.int32, (BF16_ROWS, KEY_TILE), 0)
    ones_rows = jnp.where(row == 0, 1.0, 0.0).astype(BF16)
    for g in range(N_KV_HEADS):
        lo = ATTN_WIDTH + KV_WIDTH + g * HEAD_DIM
        vg = tr[lo:lo + HEAD_DIM, :].astype(BF16)
        for c in range(n_tok // KEY_TILE):
            vt_ref[0, g, c, 0:HEAD_DIM, :] = vg[:, c * KEY_TILE:(c + 1) * KEY_TILE]
            vt_ref[0, g, c, HEAD_DIM:, :] = ones_rows


def _in_proj(x, g1, w_u, w_tr, qg, kg, tables):
    B, S, _ = x.shape
    T = PROJ_TILE
    assert S % T == 0 and T % KEY_TILE == 0 and T % Q_SUB == 0
    assert N_Q_HEADS == SUBLANES and N_KV_HEADS <= SUBLANES
    nt = S // T
    const = lambda b, i: (0, 0)
    cos_t, sin_t = tables
    return pl.pallas_call(
        _in_proj_kernel,
        grid=(B, nt),
        in_specs=[
            pl.BlockSpec((1, T, D_MODEL), lambda b, i: (b, i, 0)),
            pl.BlockSpec((1, D_MODEL), const),
            pl.BlockSpec((D_MODEL, POOL_WIDTH), const),
            pl.BlockSpec((TR_WIDTH, D_MODEL), const),
            pl.BlockSpec((HEAD_DIM, 1), const),
            pl.BlockSpec((HEAD_DIM, 1), const),
            pl.BlockSpec((HEAD_DIM, T), lambda b, i: (0, i)),
            pl.BlockSpec((HEAD_DIM, T), lambda b, i: (0, i)),
        ],
        out_specs=[
            pl.BlockSpec((1, N_Q_HEADS, T // Q_SUB, HEAD_DIM, Q_SUB), lambda b, i: (b, 0, i, 0, 0)),
            pl.BlockSpec((1, N_Q_HEADS, T // Q_SUB, 1, Q_SUB), lambda b, i: (b, 0, i, 0, 0)),
            pl.BlockSpec((1, 1, SUBLANES, LANES), lambda b, i: (b, i, 0, 0)),
            pl.BlockSpec((1, N_KV_HEADS, T, HEAD_DIM), lambda b, i: (b, 0, i, 0)),
            pl.BlockSpec((1, 1, SUBLANES, LANES), lambda b, i: (b, i, 0, 0)),
            pl.BlockSpec((1, N_KV_HEADS, T // KEY_TILE, V_ROWS, KEY_TILE),
                         lambda b, i: (b, 0, i, 0, 0)),
            pl.BlockSpec((1, T, POOL_WIDTH), lambda b, i: (b, i, 0)),
        ],
        out_shape=[
            jax.ShapeDtypeStruct((B, N_Q_HEADS, S // Q_SUB, HEAD_DIM, Q_SUB), BF16),
            jax.ShapeDtypeStruct((B, N_Q_HEADS, S // Q_SUB, 1, Q_SUB), F32),
            jax.ShapeDtypeStruct((B, nt, SUBLANES, LANES), F32),
            jax.ShapeDtypeStruct((B, N_KV_HEADS, S, HEAD_DIM), BF16),
            jax.ShapeDtypeStruct((B, nt, SUBLANES, LANES), F32),
            jax.ShapeDtypeStruct((B, N_KV_HEADS, S // KEY_TILE, V_ROWS, KEY_TILE), BF16),
            jax.ShapeDtypeStruct((B, S, POOL_WIDTH), F32),
        ],
        compiler_params=pltpu.CompilerParams(
            dimension_semantics=("parallel", "parallel"),
            vmem_limit_bytes=VMEM_LIMIT_BYTES),
        name="in_proj",
    )(x, g1, w_u, w_tr, qg, kg, cos_t, sin_t)


def _attn_kernel(safe_ref, qt_ref, qn_ref, k_ref, k2_ref, vt_ref, o_ref,
                 s_ref, p_ref, m_ref, al_ref, acc_ref):
    n_kt = vt_ref.shape[2]
    n_sub = qt_ref.shape[2]
    lanes = Q_PER_KV * Q_SUB
    k2_max = jnp.concatenate([k2_ref[0, 0]] * (lanes // LANES), axis=1)

    def queries(sub):
        return jnp.concatenate([qt_ref[0, h, sub] for h in range(Q_PER_KV)], axis=1)

    def finish(sub, acc, den):
        out_t = acc / den
        out = jnp.concatenate([out_t[:, h * Q_SUB:(h + 1) * Q_SUB].T for h in range(Q_PER_KV)],
                              axis=1)
        o_ref[0, pl.ds(pl.multiple_of(sub * Q_SUB, Q_SUB), Q_SUB), :] = out.astype(o_ref.dtype)

    def shifted_by_bound():
        per_tile = KEY_TILE // STREAM_KEYS
        n_blk = n_kt * per_tile
        for sub in range(n_sub):
            qt = queries(sub)
            qn2 = jnp.concatenate([qn_ref[0, h, sub] for h in range(Q_PER_KV)], axis=1)
            bound = jnp.sqrt(qn2 * k2_max) * BOUND_MARGIN

            def scores(b):
                return jnp.dot(k_ref[0, 0, b * STREAM_KEYS:(b + 1) * STREAM_KEYS, :], qt,
                               preferred_element_type=F32)

            acc = jnp.zeros((HEAD_DIM, lanes), F32)
            den = jnp.zeros((SUBLANES, lanes), F32)
            pending = [scores(b) for b in range(min(STREAM_AHEAD, n_blk))]
            for b in range(n_blk):
                if b + STREAM_AHEAD < n_blk:
                    pending.append(scores(b + STREAM_AHEAD))
                p = jnp.exp2(pending.pop(0) - bound)
                den = den + jnp.sum(p.reshape(STREAM_KEYS // SUBLANES, SUBLANES, lanes), axis=0)
                j, part = divmod(b, per_tile)
                vt = vt_ref[0, 0, j, 0:HEAD_DIM, part * STREAM_KEYS:(part + 1) * STREAM_KEYS]
                acc = acc + jnp.dot(vt, p.astype(BF16), preferred_element_type=F32)
            finish(sub, acc, jnp.sum(den, axis=0, keepdims=True))

    def online_max():
        def one_sub_tile(sub, carry):
            qt = queries(sub)
            m_ref[...] = jnp.full(m_ref.shape, -1e30, F32)
            acc_ref[...] = jnp.zeros(acc_ref.shape, F32)

            def scores(j):
                s_ref[j % 2] = jnp.dot(k_ref[0, 0, j * KEY_TILE:(j + 1) * KEY_TILE, :], qt,
                                       preferred_element_type=F32)

            def softmax(j):
                s = s_ref[j % 2]
                m_old = m_ref[...]
                m_new = jnp.maximum(m_old, jnp.max(s, axis=0, keepdims=True))
                al_ref[j % 2] = jnp.exp2(m_old - m_new)
                m_ref[...] = m_new
                p_ref[j % 2] = jnp.exp2(s - m_new).astype(BF16)

            def accumulate(j):
                pv = jnp.dot(vt_ref[0, 0, j], p_ref[j % 2], preferred_element_type=F32)
                acc_ref[...] = al_ref[j % 2] * acc_ref[...] + pv

            for t in range(n_kt + 2):
                if t < n_kt:
                    scores(t)
                if 1 <= t <= n_kt:
                    softmax(t - 1)
                if t >= 2:
                    accumulate(t - 2)
            acc = acc_ref[...]
            finish(sub, acc[0:HEAD_DIM], acc[HEAD_DIM:HEAD_DIM + 1])
            return carry

        jax.lax.fori_loop(0, n_sub, one_sub_tile, 0)

    jax.lax.cond(safe_ref[pl.program_id(0), pl.program_id(1)] != 0, shifted_by_bound, online_max)


def _attention(qt, qn2, q2_tiles, k, k2_tiles, vt):
    B, _, n_qs, _, _ = qt.shape
    S = n_qs * Q_SUB
    n_kt = vt.shape[2]
    n_sub = max(1, min(UNROLLED_PASS_TILES // n_kt, n_qs))
    q_tile = n_sub * Q_SUB
    assert S % q_tile == 0
    lanes = Q_PER_KV * Q_SUB

    q2_max = jnp.max(q2_tiles[:, :, :, 0], axis=1).reshape(B, N_KV_HEADS, Q_PER_KV).max(axis=-1)
    k2_max = jnp.max(k2_tiles[:, :, :N_KV_HEADS, 0], axis=1)
    safe = (jnp.sqrt(q2_max * k2_max) * BOUND_MARGIN <= MAX_BOUND_SHIFT).astype(jnp.int32)
    k2_lanes = jnp.broadcast_to(k2_max[:, :, None, None], (B, N_KV_HEADS, 1, LANES))

    grid_spec = pltpu.PrefetchScalarGridSpec(
        num_scalar_prefetch=1,
        grid=(B, N_KV_HEADS, S // q_tile),
        in_specs=[
            pl.BlockSpec((1, Q_PER_KV, n_sub, HEAD_DIM, Q_SUB),
                         lambda b, g, i, safe: (b, g, i, 0, 0)),
            pl.BlockSpec((1, Q_PER_KV, n_sub, 1, Q_SUB), lambda b, g, i, safe: (b, g, i, 0, 0)),
            pl.BlockSpec((1, 1, S, HEAD_DIM), lambda b, g, i, safe: (b, g, 0, 0)),
            pl.BlockSpec((1, 1, 1, LANES), lambda b, g, i, safe: (b, g, 0, 0)),
            pl.BlockSpec((1, 1, n_kt, V_ROWS, KEY_TILE), lambda b, g, i, safe: (b, g, 0, 0, 0)),
        ],
        out_specs=pl.BlockSpec((1, q_tile, Q_PER_KV * HEAD_DIM), lambda b, g, i, safe: (b, i, g)),
        scratch_shapes=[
            pltpu.VMEM((2, KEY_TILE, lanes), F32),
            pltpu.VMEM((2, KEY_TILE, lanes), BF16),
            pltpu.VMEM((1, lanes), F32),
            pltpu.VMEM((2, 1, lanes), F32),
            pltpu.VMEM((V_ROWS, lanes), F32),
        ],
    )
    return pl.pallas_call(
        _attn_kernel,
        grid_spec=grid_spec,
        out_shape=jax.ShapeDtypeStruct((B, S, ATTN_WIDTH), BF16),
        compiler_params=pltpu.CompilerParams(
            dimension_semantics=("parallel", "parallel", "arbitrary"),
            vmem_limit_bytes=VMEM_LIMIT_BYTES),
        name="attention",
    )(safe, qt, qn2, k, k2_lanes, vt)


def _mix_mlp_kernel(x_ref, a_ref, u_ref, up_ref, un_ref, wp_ref, ps_ref, wo_ref, g2_ref,
                    wu_ref, wd_ref, y_ref):
    i = pl.program_id(1)
    nt = pl.num_programs(1)
    T = x_ref.shape[1]
    S = T * nt
    n_ext = T + 2 * POOL_HALO

    u = u_ref[0]
    ext = jnp.concatenate([jnp.where(i > 0, up_ref[0], 0.0), u,
                           jnp.where(i < nt - 1, un_ref[0], 0.0)], axis=0)

    def rows_from(v, k):
        return pltpu.roll(v, (-k) % n_ext, 0) if k % n_ext else v

    row = jax.lax.broadcasted_iota(jnp.int32, (POOL_HALO, 1), 0)
    t_first = i * T + row
    t_last = i * T + (T - POOL_HALO) + row

    centred = []
    for g, win in enumerate(POOL_WINDOWS):
        sl = slice(g * POOL_GROUP_DIM, (g + 1) * POOL_GROUP_DIM)
        half = win // 2
        run, length = ext[:, sl], 1
        while length < win:
            run = run + rows_from(run, -length)
            length *= 2
        wsum = rows_from(run, half - 1)[POOL_HALO:POOL_HALO + T]

        def clipped(tok):
            return (jnp.minimum(tok + half, S) - jnp.maximum(tok - half, 0)).astype(F32)

        ug = u[:, sl]
        inner = slice(POOL_HALO, T - POOL_HALO)
        centred.append(jnp.concatenate([
            wsum[:POOL_HALO] / clipped(t_first) - ug[:POOL_HALO],
            wsum[inner] * (1.0 / win) - ug[inner],
            wsum[T - POOL_HALO:] / clipped(t_last) - ug[T - POOL_HALO:]], axis=0).astype(BF16))

    mixed = [a_ref[0]]
    for pair in range(N_POOL_GROUPS // POOL_PAIR):
        d = jnp.concatenate(centred[pair * POOL_PAIR:(pair + 1) * POOL_PAIR], axis=1)
        z = jnp.dot(d, wp_ref[pair], preferred_element_type=F32)
        z = z * ps_ref[:, pair * MXU_DIM:(pair + 1) * MXU_DIM]
        mixed.append(z.astype(BF16))
    mix = jnp.dot(jnp.concatenate(mixed, axis=1), wo_ref[...], preferred_element_type=F32)

    x1 = x_ref[0] + mix
    ms = jnp.mean(x1 * x1, axis=-1, keepdims=True)
    h2 = (x1 * jax.lax.rsqrt(ms + EPS) * g2_ref[...]).astype(BF16)
    acc = x1
    for c in range(D_FF // FF_CHUNK):
        sl = slice(c * FF_CHUNK, (c + 1) * FF_CHUNK)
        f = jnp.dot(h2, wu_ref[:, sl], preferred_element_type=F32)
        f = jnp.square(jnp.maximum(f, 0.0)).astype(BF16)
        acc = acc + jnp.dot(f, wd_ref[sl, :], preferred_element_type=F32)
    y_ref[0] = acc


def _mix_mlp(x, a, u, w_pool_bd, pool_scale, w_out, g2, w_up, w_down):
    B, S, _ = x.shape
    T = SEQ_TILE
    nt = S // T
    hb = T // POOL_HALO
    n_hb = S // POOL_HALO
    const2 = lambda b, i: (0, 0)
    resident = functools.partial(pl.BlockSpec, pipeline_mode=pl.Buffered(1))
    return pl.pallas_call(
        _mix_mlp_kernel,
        grid=(B, nt),
        in_specs=[
            pl.BlockSpec((1, T, D_MODEL), lambda b, i: (b, i, 0)),
            pl.BlockSpec((1, T, ATTN_WIDTH), lambda b, i: (b, i, 0)),
            pl.BlockSpec((1, T, POOL_WIDTH), lambda b, i: (b, i, 0)),
            pl.BlockSpec((1, POOL_HALO, POOL_WIDTH),
                         lambda b, i: (b, jnp.maximum(i * hb - 1, 0), 0)),
            pl.BlockSpec((1, POOL_HALO, POOL_WIDTH),
                         lambda b, i: (b, jnp.minimum((i + 1) * hb, n_hb - 1), 0)),
            resident((N_POOL_GROUPS // POOL_PAIR, MXU_DIM, MXU_DIM), lambda b, i: (0, 0, 0)),
            pl.BlockSpec((1, POOL_WIDTH), const2),
            resident((D_MODEL, D_MODEL), const2),
            pl.BlockSpec((1, D_MODEL), const2),
            resident((D_MODEL, D_FF), const2),
            resident((D_FF, D_MODEL), const2),
        ],
        out_specs=pl.BlockSpec((1, T, D_MODEL), lambda b, i: (b, i, 0)),
        out_shape=jax.ShapeDtypeStruct((B, S, D_MODEL), F32),
        compiler_params=pltpu.CompilerParams(
            dimension_semantics=("parallel", "parallel"),
            vmem_limit_bytes=VMEM_LIMIT_BYTES),
        name="mix_mlp",
    )(x, a, u, u, u, w_pool_bd, pool_scale, w_out, g2, w_up, w_down)


def _layer(x, g1, w_u, w_tr, qg, kg, w_pool_bd, pool_scale, w_out, g2, w_up, w_down, tables):
    qt, qn2, q2_tiles, k, k2_tiles, vt, u = _in_proj(x, g1, w_u, w_tr, qg, kg, tables)
    a = _attention(qt, qn2, q2_tiles, k, k2_tiles, vt)
    return _mix_mlp(x, a, u, w_pool_bd, pool_scale, w_out, g2, w_up, w_down)


def _trunk(x, layers):
    tables = _rope_tables(x.shape[1])
    for params in layers:
        x = _layer(x, *params, tables)
    return x


def _pool_block_diagonal(w_pool):
    c = POOL_GROUP_DIM
    out = jnp.zeros((N_POOL_GROUPS // POOL_PAIR, MXU_DIM, MXU_DIM), w_pool.dtype)
    for g in range(N_POOL_GROUPS):
        pair, slot = divmod(g, POOL_PAIR)
        out = out.at[pair, slot * c:(slot + 1) * c, slot * c:(slot + 1) * c].set(w_pool[g])
    return out


def kernel(x_prompt, x_sample, norm1_g, w_in, q_norm_g, k_norm_g, w_pool, pool_scale, w_out,
           norm2_g, w_up, w_down):
    depth = w_in.shape[0]
    qkv_end = ATTN_WIDTH + 2 * KV_WIDTH
    layers = []
    for l in range(depth):
        w = w_in[l].astype(BF16)
        layers.append((
            norm1_g[l][None, :],
            w[:, qkv_end:],
            w[:, :qkv_end].T,
            q_norm_g[l][:, None],
            k_norm_g[l][:, None],
            _pool_block_diagonal(w_pool[l].astype(BF16)),
            pool_scale[l][None, :],
            w_out[l].astype(BF16),
            norm2_g[l][None, :],
            w_up[l].astype(BF16),
            w_down[l].astype(BF16),
        ))
    return (_trunk(x_prompt, layers), _trunk(x_sample, layers))
```

```python
import functools
import math

import jax
import jax.numpy as jnp
from jax.experimental import pallas as pl
from jax.experimental.pallas import tpu as pltpu

D_MODEL = 1024
GRID_W = 64
HEAD_DIM = 64
N_Q_HEADS = 8
N_KV_HEADS = 2
Q_PER_KV = N_Q_HEADS // N_KV_HEADS
ATTN_WIDTH = N_Q_HEADS * HEAD_DIM
KV_WIDTH = N_KV_HEADS * HEAD_DIM
POOL_WINDOWS = (2, 4, 8, 16)
N_POOL_GROUPS = len(POOL_WINDOWS)
POOL_WIDTH = D_MODEL - ATTN_WIDTH
POOL_GROUP_DIM = POOL_WIDTH // N_POOL_GROUPS
D_FF = 4 * D_MODEL
ROPE_THETA = 10000.0
EPS = 1e-6
N_FREQ = HEAD_DIM // 4

LANES = 128
SUBLANES = 8
BF16_ROWS = 16
MXU_DIM = 256
MXU_ROWS = 256
POOL_HALO = 8
assert all(w & (w - 1) == 0 and w // 2 <= POOL_HALO for w in POOL_WINDOWS)
POOL_PAIR = MXU_DIM // POOL_GROUP_DIM
assert N_POOL_GROUPS % POOL_PAIR == 0
VMEM_LIMIT_BYTES = 56 * 1024 * 1024

PROJ_TILE = 1024
SEQ_TILE = 1024
KEY_TILE = 512
Q_SUB = 128
UNROLLED_PASS_TILES = 64
FF_CHUNK = 1024
MLP_ROW_BLOCKS = 4
V_ROWS = HEAD_DIM + BF16_ROWS
MAX_BOUND_SHIFT = 30.0
BOUND_MARGIN = 1.02
STREAM_KEYS = 256
STREAM_AHEAD = 2

F32 = jnp.float32
BF16 = jnp.bfloat16

TR_WIDTH = ATTN_WIDTH + 2 * KV_WIDTH


def _rope_tables(seq_len):
    rows = seq_len // GRID_W
    row_ids = jnp.broadcast_to(jnp.arange(rows, dtype=F32)[:, None], (rows, GRID_W)).reshape(-1)
    col_ids = jnp.broadcast_to(jnp.arange(GRID_W, dtype=F32)[None, :], (rows, GRID_W)).reshape(-1)
    inv_freq = ROPE_THETA ** (-jnp.arange(N_FREQ, dtype=F32) / N_FREQ)
    ang = jnp.stack([row_ids[:, None] * inv_freq, col_ids[:, None] * inv_freq], axis=1)
    cos, sin = jnp.cos(ang), jnp.sin(ang)
    cos_h = jnp.stack([cos, cos], axis=2).reshape(seq_len, HEAD_DIM)
    sin_sg = jnp.stack([-sin, sin], axis=2).reshape(seq_len, HEAD_DIM)
    return cos_h.T, sin_sg.T


def _in_proj_kernel(x_ref, g1_ref, w_u_ref, w_tr_ref, qg_ref, kg_ref, cos_t_ref, sin_t_ref,
                    qt_ref, qn_ref, q2_ref, k_ref, k2_ref, vt_ref, u_ref):
    x = x_ref[0]
    ms = jnp.mean(x * x, axis=-1, keepdims=True)
    h = (x * jax.lax.rsqrt(ms + EPS) * g1_ref[...]).astype(BF16)
    n_tok = h.shape[0]
    for r in range(0, n_tok, MXU_ROWS):
        u_ref[0, r:r + MXU_ROWS, :] = jnp.dot(h[r:r + MXU_ROWS], w_u_ref[...],
                                              preferred_element_type=F32)
    tr = jnp.concatenate(
        [jax.lax.dot_general(w_tr_ref[r:r + MXU_ROWS, :], h, (((1,), (1,)), ((), ())),
                             preferred_element_type=F32)
         for r in range(0, TR_WIDTH, MXU_ROWS)], axis=0)
    cos_t, sin_t = cos_t_ref[...], sin_t_ref[...]

    def norm_rope(t, gain):
        tn = t * jax.lax.rsqrt(jnp.mean(t * t, axis=0, keepdims=True) + EPS) * gain
        partner = jnp.concatenate([tn[N_FREQ:2 * N_FREQ], tn[0:N_FREQ],
                                   tn[3 * N_FREQ:4 * N_FREQ], tn[2 * N_FREQ:3 * N_FREQ]], axis=0)
        return tn * cos_t + partner * sin_t, jnp.sum(tn * tn, axis=0, keepdims=True)

    def lane_max_row(v):
        return jnp.broadcast_to(jnp.max(v, axis=1, keepdims=True), (1, LANES))

    q_gain = qg_ref[...] * (HEAD_DIM ** -0.5 * math.log2(math.e))
    q2_rows = []
    for hd in range(N_Q_HEADS):
        qr, qn2 = norm_rope(tr[hd * HEAD_DIM:(hd + 1) * HEAD_DIM, :], q_gain)
        qr = qr.astype(BF16)
        for c in range(n_tok // Q_SUB):
            qt_ref[0, hd, c] = qr[:, c * Q_SUB:(c + 1) * Q_SUB]
            qn_ref[0, hd, c] = qn2[:, c * Q_SUB:(c + 1) * Q_SUB]
        q2_rows.append(lane_max_row(qn2))
    q2_ref[0, 0] = jnp.concatenate(q2_rows, axis=0)

    k_gain = kg_ref[...]
    kr_heads, k2_rows = [], []
    for g in range(N_KV_HEADS):
        lo = ATTN_WIDTH + g * HEAD_DIM
        kr, kn2 = norm_rope(tr[lo:lo + HEAD_DIM, :], k_gain)
        kr_heads.append(kr)
        k2_rows.append(lane_max_row(kn2))
    k_tok = jnp.concatenate(kr_heads, axis=0).T
    for g in range(N_KV_HEADS):
        k_ref[0, g] = k_tok[:, g * HEAD_DIM:(g + 1) * HEAD_DIM].astype(BF16)
    k2_rows.append(jnp.zeros((SUBLANES - N_KV_HEADS, LANES), F32))
    k2_ref[0, 0] = jnp.concatenate(k2_rows, axis=0)

    row = jax.lax.broadcasted_iota(jnp.int32, (BF16_ROWS, KEY_TILE), 0)
    ones_rows = jnp.where(row == 0, 1.0, 0.0).astype(BF16)
    for g in range(N_KV_HEADS):
        lo = ATTN_WIDTH + KV_WIDTH + g * HEAD_DIM
        vg = tr[lo:lo + HEAD_DIM, :].astype(BF16)
        for c in range(n_tok // KEY_TILE):
            vt_ref[0, g, c, 0:HEAD_DIM, :] = vg[:, c * KEY_TILE:(c + 1) * KEY_TILE]
            vt_ref[0, g, c, HEAD_DIM:, :] = ones_rows


def _in_proj(x, g1, w_u, w_tr, qg, kg, tables):
    B, S, _ = x.shape
    T = PROJ_TILE
    assert S % T == 0 and T % KEY_TILE == 0 and T % Q_SUB == 0
    assert N_Q_HEADS == SUBLANES and N_KV_HEADS <= SUBLANES
    nt = S // T
    const = lambda b, i: (0, 0)
    cos_t, sin_t = tables
    return pl.pallas_call(
        _in_proj_kernel,
        grid=(B, nt),
        in_specs=[
            pl.BlockSpec((1, T, D_MODEL), lambda b, i: (b, i, 0)),
            pl.BlockSpec((1, D_MODEL), const),
            pl.BlockSpec((D_MODEL, POOL_WIDTH), const),
            pl.BlockSpec((TR_WIDTH, D_MODEL), const),
            pl.BlockSpec((HEAD_DIM, 1), const),
            pl.BlockSpec((HEAD_DIM, 1), const),
            pl.BlockSpec((HEAD_DIM, T), lambda b, i: (0, i)),
            pl.BlockSpec((HEAD_DIM, T), lambda b, i: (0, i)),
        ],
        out_specs=[
            pl.BlockSpec((1, N_Q_HEADS, T // Q_SUB, HEAD_DIM, Q_SUB), lambda b, i: (b, 0, i, 0, 0)),
            pl.BlockSpec((1, N_Q_HEADS, T // Q_SUB, 1, Q_SUB), lambda b, i: (b, 0, i, 0, 0)),
            pl.BlockSpec((1, 1, SUBLANES, LANES), lambda b, i: (b, i, 0, 0)),
            pl.BlockSpec((1, N_KV_HEADS, T, HEAD_DIM), lambda b, i: (b, 0, i, 0)),
            pl.BlockSpec((1, 1, SUBLANES, LANES), lambda b, i: (b, i, 0, 0)),
            pl.BlockSpec((1, N_KV_HEADS, T // KEY_TILE, V_ROWS, KEY_TILE),
                         lambda b, i: (b, 0, i, 0, 0)),
            pl.BlockSpec((1, T, POOL_WIDTH), lambda b, i: (b, i, 0)),
        ],
        out_shape=[
            jax.ShapeDtypeStruct((B, N_Q_HEADS, S // Q_SUB, HEAD_DIM, Q_SUB), BF16),
            jax.ShapeDtypeStruct((B, N_Q_HEADS, S // Q_SUB, 1, Q_SUB), F32),
            jax.ShapeDtypeStruct((B, nt, SUBLANES, LANES), F32),
            jax.ShapeDtypeStruct((B, N_KV_HEADS, S, HEAD_DIM), BF16),
            jax.ShapeDtypeStruct((B, nt, SUBLANES, LANES), F32),
            jax.ShapeDtypeStruct((B, N_KV_HEADS, S // KEY_TILE, V_ROWS, KEY_TILE), BF16),
            jax.ShapeDtypeStruct((B, S, POOL_WIDTH), F32),
        ],
        compiler_params=pltpu.CompilerParams(
            dimension_semantics=("parallel", "parallel"),
            vmem_limit_bytes=VMEM_LIMIT_BYTES),
        name="in_proj",
    )(x, g1, w_u, w_tr, qg, kg, cos_t, sin_t)


def _attn_kernel(safe_ref, qt_ref, qn_ref, k_ref, k2_ref, vt_ref, o_ref,
                 s_ref, p_ref, m_ref, al_ref, acc_ref):
    n_kt = vt_ref.shape[2]
    n_sub = qt_ref.shape[2]
    lanes = Q_PER_KV * Q_SUB
    k2_max = jnp.concatenate([k2_ref[0, 0]] * (lanes // LANES), axis=1)

    def queries(sub):
        return jnp.concatenate([qt_ref[0, h, sub] for h in range(Q_PER_KV)], axis=1)

    def finish(sub, acc, den):
        out_t = acc / den
        pairs = [jnp.concatenate([out_t[:, h * Q_SUB:(h + 1) * Q_SUB] for h in (p, p + 1)], axis=0).T
                 for p in range(0, Q_PER_KV, 2)]
        out = jnp.concatenate(pairs, axis=1)
        o_ref[0, pl.ds(pl.multiple_of(sub * Q_SUB, Q_SUB), Q_SUB), :] = out.astype(o_ref.dtype)

    def shifted_by_bound():
        per_tile = KEY_TILE // STREAM_KEYS
        n_blk = n_kt * per_tile
        qts = {}

        def scores(item):
            sub, b = item
            if sub not in qts:
                qts[sub] = queries(sub)
            return jnp.dot(k_ref[0, 0, b * STREAM_KEYS:(b + 1) * STREAM_KEYS, :], qts[sub],
                           preferred_element_type=F32)

        items = [(sub, b) for sub in range(n_sub) for b in range(n_blk)]
        pending = [scores(it) for it in items[:STREAM_AHEAD]]
        for n, (sub, b) in enumerate(items):
            if n + STREAM_AHEAD < len(items):
                pending.append(scores(items[n + STREAM_AHEAD]))
            if b == 0:
                qn2 = jnp.concatenate([qn_ref[0, h, sub] for h in range(Q_PER_KV)], axis=1)
                bound = jnp.sqrt(qn2 * k2_max) * BOUND_MARGIN
                acc = jnp.zeros((HEAD_DIM, lanes), F32)
                den = jnp.zeros((SUBLANES, lanes), F32)
            p = jnp.exp2(pending.pop(0) - bound)
            den = den + jnp.sum(p.reshape(STREAM_KEYS // SUBLANES, SUBLANES, lanes), axis=0)
            j, part = divmod(b, per_tile)
            vt = vt_ref[0, 0, j, 0:HEAD_DIM, part * STREAM_KEYS:(part + 1) * STREAM_KEYS]
            acc = acc + jnp.dot(vt, p.astype(BF16), preferred_element_type=F32)
            if b == n_blk - 1:
                finish(sub, acc, jnp.sum(den, axis=0, keepdims=True))

    def online_max():
        def one_sub_tile(sub, carry):
            qt = queries(sub)
            m_ref[...] = jnp.full(m_ref.shape, -1e30, F32)
            acc_ref[...] = jnp.zeros(acc_ref.shape, F32)

            def scores(j):
                s_ref[j % 2] = jnp.dot(k_ref[0, 0, j * KEY_TILE:(j + 1) * KEY_TILE, :], qt,
                                       preferred_element_type=F32)

            def softmax(j):
                s = s_ref[j % 2]
                m_old = m_ref[...]
                m_new = jnp.maximum(m_old, jnp.max(s, axis=0, keepdims=True))
                al_ref[j % 2] = jnp.exp2(m_old - m_new)
                m_ref[...] = m_new
                p_ref[j % 2] = jnp.exp2(s - m_new).astype(BF16)

            def accumulate(j):
                pv = jnp.dot(vt_ref[0, 0, j], p_ref[j % 2], preferred_element_type=F32)
                acc_ref[...] = al_ref[j % 2] * acc_ref[...] + pv

            for t in range(n_kt + 2):
                if t < n_kt:
                    scores(t)
                if 1 <= t <= n_kt:
                    softmax(t - 1)
                if t >= 2:
                    accumulate(t - 2)
            acc = acc_ref[...]
            finish(sub, acc[0:HEAD_DIM], acc[HEAD_DIM:HEAD_DIM + 1])
            return carry

        jax.lax.fori_loop(0, n_sub, one_sub_tile, 0)

    jax.lax.cond(safe_ref[pl.program_id(0), pl.program_id(1)] != 0, shifted_by_bound, online_max)


def _attention(qt, qn2, q2_tiles, k, k2_tiles, vt):
    B, _, n_qs, _, _ = qt.shape
    S = n_qs * Q_SUB
    n_kt = vt.shape[2]
    n_sub = max(1, min(UNROLLED_PASS_TILES // n_kt, n_qs))
    q_tile = n_sub * Q_SUB
    assert S % q_tile == 0
    lanes = Q_PER_KV * Q_SUB

    q2_max = jnp.max(q2_tiles[:, :, :, 0], axis=1).reshape(B, N_KV_HEADS, Q_PER_KV).max(axis=-1)
    k2_max = jnp.max(k2_tiles[:, :, :N_KV_HEADS, 0], axis=1)
    safe = (jnp.sqrt(q2_max * k2_max) * BOUND_MARGIN <= MAX_BOUND_SHIFT).astype(jnp.int32)
    k2_lanes = jnp.broadcast_to(k2_max[:, :, None, None], (B, N_KV_HEADS, 1, LANES))

    grid_spec = pltpu.PrefetchScalarGridSpec(
        num_scalar_prefetch=1,
        grid=(B, N_KV_HEADS, S // q_tile),
        in_specs=[
            pl.BlockSpec((1, Q_PER_KV, n_sub, HEAD_DIM, Q_SUB),
                         lambda b, g, i, safe: (b, g, i, 0, 0)),
            pl.BlockSpec((1, Q_PER_KV, n_sub, 1, Q_SUB), lambda b, g, i, safe: (b, g, i, 0, 0)),
            pl.BlockSpec((1, 1, S, HEAD_DIM), lambda b, g, i, safe: (b, g, 0, 0)),
            pl.BlockSpec((1, 1, 1, LANES), lambda b, g, i, safe: (b, g, 0, 0)),
            pl.BlockSpec((1, 1, n_kt, V_ROWS, KEY_TILE), lambda b, g, i, safe: (b, g, 0, 0, 0)),
        ],
        out_specs=pl.BlockSpec((1, q_tile, Q_PER_KV * HEAD_DIM), lambda b, g, i, safe: (b, i, g)),
        scratch_shapes=[
            pltpu.VMEM((2, KEY_TILE, lanes), F32),
            pltpu.VMEM((2, KEY_TILE, lanes), BF16),
            pltpu.VMEM((1, lanes), F32),
            pltpu.VMEM((2, 1, lanes), F32),
            pltpu.VMEM((V_ROWS, lanes), F32),
        ],
    )
    return pl.pallas_call(
        _attn_kernel,
        grid_spec=grid_spec,
        out_shape=jax.ShapeDtypeStruct((B, S, ATTN_WIDTH), BF16),
        compiler_params=pltpu.CompilerParams(
            dimension_semantics=("parallel", "parallel", "arbitrary"),
            vmem_limit_bytes=VMEM_LIMIT_BYTES),
        name="attention",
    )(safe, qt, qn2, k, k2_lanes, vt)


def _mix_mlp_kernel(x_ref, a_ref, u_ref, up_ref, un_ref, wp_ref, ps_ref, wo_ref, g2_ref,
                    wu_ref, wd_ref, y_ref):
    i = pl.program_id(1)
    nt = pl.num_programs(1)
    T = x_ref.shape[1]
    S = T * nt
    n_ext = T + 2 * POOL_HALO

    u = u_ref[0]
    ext = jnp.concatenate([jnp.where(i > 0, up_ref[0], 0.0), u,
                           jnp.where(i < nt - 1, un_ref[0], 0.0)], axis=0)

    def rows_from(v, k):
        return pltpu.roll(v, (-k) % n_ext, 0) if k % n_ext else v

    row = jax.lax.broadcasted_iota(jnp.int32, (POOL_HALO, 1), 0)
    t_first = i * T + row
    t_last = i * T + (T - POOL_HALO) + row

    centred = []
    for g, win in enumerate(POOL_WINDOWS):
        sl = slice(g * POOL_GROUP_DIM, (g + 1) * POOL_GROUP_DIM)
        half = win // 2
        run, length = ext[:, sl], 1
        while length < win:
            run = run + rows_from(run, -length)
            length *= 2
        wsum = rows_from(run, half - 1)[POOL_HALO:POOL_HALO + T]

        def clipped(tok):
            return (jnp.minimum(tok + half, S) - jnp.maximum(tok - half, 0)).astype(F32)

        ug = u[:, sl]
        inner = slice(POOL_HALO, T - POOL_HALO)
        centred.append(jnp.concatenate([
            wsum[:POOL_HALO] / clipped(t_first) - ug[:POOL_HALO],
            wsum[inner] * (1.0 / win) - ug[inner],
            wsum[T - POOL_HALO:] / clipped(t_last) - ug[T - POOL_HALO:]], axis=0).astype(BF16))

    rows_per = T // MLP_ROW_BLOCKS
    residual, normed = [], []
    for r in range(MLP_ROW_BLOCKS):
        rows = slice(r * rows_per, (r + 1) * rows_per)
        mixed = [a_ref[0, rows, :]]
        for pair in range(N_POOL_GROUPS // POOL_PAIR):
            d = jnp.concatenate([c[rows] for c in centred[pair * POOL_PAIR:(pair + 1) * POOL_PAIR]],
                                axis=1)
            z = jnp.dot(d, wp_ref[pair], preferred_element_type=F32)
            z = z * ps_ref[:, pair * MXU_DIM:(pair + 1) * MXU_DIM]
            mixed.append(z.astype(BF16))
        mix = jnp.dot(jnp.concatenate(mixed, axis=1), wo_ref[...], preferred_element_type=F32)
        x1 = x_ref[0, rows, :] + mix
        ms = jnp.mean(x1 * x1, axis=-1, keepdims=True)
        residual.append(x1)
        normed.append((x1 * jax.lax.rsqrt(ms + EPS) * g2_ref[...]).astype(BF16))

    for r in range(MLP_ROW_BLOCKS):
        acc = residual[r]
        for c in range(D_FF // FF_CHUNK):
            sl = slice(c * FF_CHUNK, (c + 1) * FF_CHUNK)
            f = jnp.dot(normed[r], wu_ref[:, sl], preferred_element_type=F32)
            f = jnp.square(jnp.maximum(f, 0.0)).astype(BF16)
            acc = acc + jnp.dot(f, wd_ref[sl, :], preferred_element_type=F32)
        y_ref[0, r * rows_per:(r + 1) * rows_per, :] = acc


def _mix_mlp(x, a, u, w_pool_bd, pool_scale, w_out, g2, w_up, w_down):
    B, S, _ = x.shape
    T = SEQ_TILE
    nt = S // T
    hb = T // POOL_HALO
    n_hb = S // POOL_HALO
    const2 = lambda b, i: (0, 0)
    resident = functools.partial(pl.BlockSpec, pipeline_mode=pl.Buffered(1))
    return pl.pallas_call(
        _mix_mlp_kernel,
        grid=(B, nt),
        in_specs=[
            pl.BlockSpec((1, T, D_MODEL), lambda b, i: (b, i, 0)),
            pl.BlockSpec((1, T, ATTN_WIDTH), lambda b, i: (b, i, 0)),
            pl.BlockSpec((1, T, POOL_WIDTH), lambda b, i: (b, i, 0)),
            pl.BlockSpec((1, POOL_HALO, POOL_WIDTH),
                         lambda b, i: (b, jnp.maximum(i * hb - 1, 0), 0)),
            pl.BlockSpec((1, POOL_HALO, POOL_WIDTH),
                         lambda b, i: (b, jnp.minimum((i + 1) * hb, n_hb - 1), 0)),
            resident((N_POOL_GROUPS // POOL_PAIR, MXU_DIM, MXU_DIM), lambda b, i: (0, 0, 0)),
            pl.BlockSpec((1, POOL_WIDTH), const2),
            resident((D_MODEL, D_MODEL), const2),
            pl.BlockSpec((1, D_MODEL), const2),
            resident((D_MODEL, D_FF), const2),
            resident((D_FF, D_MODEL), const2),
        ],
        out_specs=pl.BlockSpec((1, T, D_MODEL), lambda b, i: (b, i, 0)),
        out_shape=jax.ShapeDtypeStruct((B, S, D_MODEL), F32),
        compiler_params=pltpu.CompilerParams(
            dimension_semantics=("parallel", "parallel"),
            vmem_limit_bytes=VMEM_LIMIT_BYTES),
        name="mix_mlp",
    )(x, a, u, u, u, w_pool_bd, pool_scale, w_out, g2, w_up, w_down)


def _layer(x, g1, w_u, w_tr, qg, kg, w_pool_bd, pool_scale, w_out, g2, w_up, w_down, tables):
    qt, qn2, q2_tiles, k, k2_tiles, vt, u = _in_proj(x, g1, w_u, w_tr, qg, kg, tables)
    a = _attention(qt, qn2, q2_tiles, k, k2_tiles, vt)
    return _mix_mlp(x, a, u, w_pool_bd, pool_scale, w_out, g2, w_up, w_down)


def _trunk(x, layers):
    tables = _rope_tables(x.shape[1])
    for params in layers:
        x = _layer(x, *params, tables)
    return x


def _pool_block_diagonal(w_pool):
    c = POOL_GROUP_DIM
    out = jnp.zeros((N_POOL_GROUPS // POOL_PAIR, MXU_DIM, MXU_DIM), w_pool.dtype)
    for g in range(N_POOL_GROUPS):
        pair, slot = divmod(g, POOL_PAIR)
        out = out.at[pair, slot * c:(slot + 1) * c, slot * c:(slot + 1) * c].set(w_pool[g])
    return out


def kernel(x_prompt, x_sample, norm1_g, w_in, q_norm_g, k_norm_g, w_pool, pool_scale, w_out,
           norm2_g, w_up, w_down):
    depth = w_in.shape[0]
    qkv_end = ATTN_WIDTH + 2 * KV_WIDTH
    layers = []
    for l in range(depth):
        w = w_in[l].astype(BF16)
        layers.append((
            norm1_g[l][None, :],
            w[:, qkv_end:],
            w[:, :qkv_end].T,
            q_norm_g[l][:, None],
            k_norm_g[l][:, None],
            _pool_block_diagonal(w_pool[l].astype(BF16)),
            pool_scale[l][None, :],
            w_out[l].astype(BF16),
            norm2_g[l][None, :],
            w_up[l].astype(BF16),
            w_down[l].astype(BF16),
        ))
    return (_trunk(x_prompt, layers), _trunk(x_sample, layers))
```

```python
import functools
import math

import jax
import jax.numpy as jnp
from jax.experimental import pallas as pl
from jax.experimental.pallas import tpu as pltpu

D_MODEL = 1024
GRID_W = 64
HEAD_DIM = 64
N_Q_HEADS = 8
N_KV_HEADS = 2
Q_PER_KV = N_Q_HEADS // N_KV_HEADS
ATTN_WIDTH = N_Q_HEADS * HEAD_DIM
KV_WIDTH = N_KV_HEADS * HEAD_DIM
POOL_WINDOWS = (2, 4, 8, 16)
N_POOL_GROUPS = len(POOL_WINDOWS)
POOL_WIDTH = D_MODEL - ATTN_WIDTH
POOL_GROUP_DIM = POOL_WIDTH // N_POOL_GROUPS
D_FF = 4 * D_MODEL
ROPE_THETA = 10000.0
EPS = 1e-6
N_FREQ = HEAD_DIM // 4

LANES = 128
SUBLANES = 8
BF16_ROWS = 16
MXU_DIM = 256
MXU_ROWS = 256
POOL_HALO = 8
assert all(w & (w - 1) == 0 and w // 2 <= POOL_HALO for w in POOL_WINDOWS)
POOL_PAIR = MXU_DIM // POOL_GROUP_DIM
assert N_POOL_GROUPS % POOL_PAIR == 0
VMEM_LIMIT_BYTES = 56 * 1024 * 1024

PROJ_TILE = 1024
SEQ_TILE = 1024
KEY_TILE = 512
Q_SUB = 128
UNROLLED_PASS_TILES = 64
FF_CHUNK = 1024
MLP_ROW_BLOCKS = 4
V_ROWS = HEAD_DIM + BF16_ROWS
MAX_BOUND_SHIFT = 30.0
BOUND_MARGIN = 1.02
STREAM_KEYS = 256
STREAM_AHEAD = 2

F32 = jnp.float32
BF16 = jnp.bfloat16

TR_WIDTH = ATTN_WIDTH + 2 * KV_WIDTH


def _rope_tables(seq_len):
    rows = seq_len // GRID_W
    row_ids = jnp.broadcast_to(jnp.arange(rows, dtype=F32)[:, None], (rows, GRID_W)).reshape(-1)
    col_ids = jnp.broadcast_to(jnp.arange(GRID_W, dtype=F32)[None, :], (rows, GRID_W)).reshape(-1)
    inv_freq = ROPE_THETA ** (-jnp.arange(N_FREQ, dtype=F32) / N_FREQ)
    ang = jnp.stack([row_ids[:, None] * inv_freq, col_ids[:, None] * inv_freq], axis=1)
    cos, sin = jnp.cos(ang), jnp.sin(ang)
    cos_h = jnp.stack([cos, cos], axis=2).reshape(seq_len, HEAD_DIM)
    sin_sg = jnp.stack([-sin, sin], axis=2).reshape(seq_len, HEAD_DIM)
    return cos_h.T, sin_sg.T


def _in_proj_kernel(x_ref, g1_ref, w_u_ref, w_tr_ref, qg_ref, kg_ref, cos_t_ref, sin_t_ref,
                    qt_ref, qn_ref, q2_ref, k_ref, k2_ref, vt_ref, u_ref):
    x = x_ref[0]
    ms = jnp.mean(x * x, axis=-1, keepdims=True)
    h = (x * jax.lax.rsqrt(ms + EPS) * g1_ref[...]).astype(BF16)
    n_tok = h.shape[0]
    for r in range(0, n_tok, MXU_ROWS):
        u_ref[0, r:r + MXU_ROWS, :] = jnp.dot(h[r:r + MXU_ROWS], w_u_ref[...],
                                              preferred_element_type=F32)
    tr = jnp.concatenate(
        [jax.lax.dot_general(w_tr_ref[r:r + MXU_ROWS, :], h, (((1,), (1,)), ((), ())),
                             preferred_element_type=F32)
         for r in range(0, TR_WIDTH, MXU_ROWS)], axis=0)
    cos_t, sin_t = cos_t_ref[...], sin_t_ref[...]

    def norm_rope(t, gain):
        tn = t * jax.lax.rsqrt(jnp.mean(t * t, axis=0, keepdims=True) + EPS) * gain
        partner = jnp.concatenate([tn[N_FREQ:2 * N_FREQ], tn[0:N_FREQ],
                                   tn[3 * N_FREQ:4 * N_FREQ], tn[2 * N_FREQ:3 * N_FREQ]], axis=0)
        return tn * cos_t + partner * sin_t, jnp.sum(tn * tn, axis=0, keepdims=True)

    def lane_max_row(v):
        return jnp.broadcast_to(jnp.max(v, axis=1, keepdims=True), (1, LANES))

    q_gain = qg_ref[...] * (HEAD_DIM ** -0.5 * math.log2(math.e))
    q2_rows = []
    for hd in range(N_Q_HEADS):
        qr, qn2 = norm_rope(tr[hd * HEAD_DIM:(hd + 1) * HEAD_DIM, :], q_gain)
        qr = qr.astype(BF16)
        for c in range(n_tok // Q_SUB):
            qt_ref[0, hd, c] = qr[:, c * Q_SUB:(c + 1) * Q_SUB]
            qn_ref[0, hd, c] = qn2[:, c * Q_SUB:(c + 1) * Q_SUB]
        q2_rows.append(lane_max_row(qn2))
    q2_ref[0, 0] = jnp.concatenate(q2_rows, axis=0)

    k_gain = kg_ref[...]
    kr_heads, k2_rows = [], []
    for g in range(N_KV_HEADS):
        lo = ATTN_WIDTH + g * HEAD_DIM
        kr, kn2 = norm_rope(tr[lo:lo + HEAD_DIM, :], k_gain)
        kr_heads.append(kr)
        k2_rows.append(lane_max_row(kn2))
    k_ref[0] = jnp.concatenate(kr_heads, axis=0).T.astype(BF16)
    k2_rows.append(jnp.zeros((SUBLANES - N_KV_HEADS, LANES), F32))
    k2_ref[0, 0] = jnp.concatenate(k2_rows, axis=0)

    row = jax.lax.broadcasted_iota(jnp.int32, (BF16_ROWS, KEY_TILE), 0)
    ones_rows = jnp.where(row == 0, 1.0, 0.0).astype(BF16)
    for g in range(N_KV_HEADS):
        lo = ATTN_WIDTH + KV_WIDTH + g * HEAD_DIM
        vg = tr[lo:lo + HEAD_DIM, :].astype(BF16)
        for c in range(n_tok // KEY_TILE):
            vt_ref[0, g, c, 0:HEAD_DIM, :] = vg[:, c * KEY_TILE:(c + 1) * KEY_TILE]
            vt_ref[0, g, c, HEAD_DIM:, :] = ones_rows


def _in_proj(x, g1, w_u, w_tr, qg, kg, tables):
    B, S, _ = x.shape
    T = PROJ_TILE
    assert S % T == 0 and T % KEY_TILE == 0 and T % Q_SUB == 0
    assert N_Q_HEADS == SUBLANES and N_KV_HEADS <= SUBLANES
    nt = S // T
    const = lambda b, i: (0, 0)
    cos_t, sin_t = tables
    return pl.pallas_call(
        _in_proj_kernel,
        grid=(B, nt),
        in_specs=[
            pl.BlockSpec((1, T, D_MODEL), lambda b, i: (b, i, 0)),
            pl.BlockSpec((1, D_MODEL), const),
            pl.BlockSpec((D_MODEL, POOL_WIDTH), const),
            pl.BlockSpec((TR_WIDTH, D_MODEL), const),
            pl.BlockSpec((HEAD_DIM, 1), const),
            pl.BlockSpec((HEAD_DIM, 1), const),
            pl.BlockSpec((HEAD_DIM, T), lambda b, i: (0, i)),
            pl.BlockSpec((HEAD_DIM, T), lambda b, i: (0, i)),
        ],
        out_specs=[
            pl.BlockSpec((1, N_Q_HEADS, T // Q_SUB, HEAD_DIM, Q_SUB), lambda b, i: (b, 0, i, 0, 0)),
            pl.BlockSpec((1, N_Q_HEADS, T // Q_SUB, 1, Q_SUB), lambda b, i: (b, 0, i, 0, 0)),
            pl.BlockSpec((1, 1, SUBLANES, LANES), lambda b, i: (b, i, 0, 0)),
            pl.BlockSpec((1, T, KV_WIDTH), lambda b, i: (b, i, 0)),
            pl.BlockSpec((1, 1, SUBLANES, LANES), lambda b, i: (b, i, 0, 0)),
            pl.BlockSpec((1, N_KV_HEADS, T // KEY_TILE, V_ROWS, KEY_TILE),
                         lambda b, i: (b, 0, i, 0, 0)),
            pl.BlockSpec((1, T, POOL_WIDTH), lambda b, i: (b, i, 0)),
        ],
        out_shape=[
            jax.ShapeDtypeStruct((B, N_Q_HEADS, S // Q_SUB, HEAD_DIM, Q_SUB), BF16),
            jax.ShapeDtypeStruct((B, N_Q_HEADS, S // Q_SUB, 1, Q_SUB), F32),
            jax.ShapeDtypeStruct((B, nt, SUBLANES, LANES), F32),
            jax.ShapeDtypeStruct((B, S, KV_WIDTH), BF16),
            jax.ShapeDtypeStruct((B, nt, SUBLANES, LANES), F32),
            jax.ShapeDtypeStruct((B, N_KV_HEADS, S // KEY_TILE, V_ROWS, KEY_TILE), BF16),
            jax.ShapeDtypeStruct((B, S, POOL_WIDTH), F32),
        ],
        compiler_params=pltpu.CompilerParams(
            dimension_semantics=("parallel", "parallel"),
            vmem_limit_bytes=VMEM_LIMIT_BYTES),
        name="in_proj",
    )(x, g1, w_u, w_tr, qg, kg, cos_t, sin_t)


def _attn_kernel(safe_ref, qt_ref, qn_ref, k_ref, k2_ref, vt_ref, o_ref,
                 s_ref, p_ref, m_ref, al_ref, acc_ref):
    n_kt = vt_ref.shape[2]
    n_sub = qt_ref.shape[2]
    lanes = Q_PER_KV * Q_SUB
    k2_max = jnp.concatenate([k2_ref[0, 0]] * (lanes // LANES), axis=1)

    head_of_row = jax.lax.broadcasted_iota(jnp.int32, (KV_WIDTH, lanes), 0) // HEAD_DIM
    own_rows = head_of_row == pl.program_id(1)

    def queries(sub):
        qt = jnp.concatenate([qt_ref[0, h, sub] for h in range(Q_PER_KV)], axis=1)
        return jnp.where(own_rows, jnp.concatenate([qt] * N_KV_HEADS, axis=0), 0)

    def finish(sub, acc, den):
        out_t = acc / den
        pairs = [jnp.concatenate([out_t[:, h * Q_SUB:(h + 1) * Q_SUB] for h in (p, p + 1)], axis=0).T
                 for p in range(0, Q_PER_KV, 2)]
        out = jnp.concatenate(pairs, axis=1)
        o_ref[0, pl.ds(pl.multiple_of(sub * Q_SUB, Q_SUB), Q_SUB), :] = out.astype(o_ref.dtype)

    def shifted_by_bound():
        per_tile = KEY_TILE // STREAM_KEYS
        n_blk = n_kt * per_tile
        qts = {}

        def scores(item):
            sub, b = item
            if sub not in qts:
                qts[sub] = queries(sub)
            return jnp.dot(k_ref[0, b * STREAM_KEYS:(b + 1) * STREAM_KEYS, :], qts[sub],
                           preferred_element_type=F32)

        items = [(sub, b) for sub in range(n_sub) for b in range(n_blk)]
        pending = [scores(it) for it in items[:STREAM_AHEAD]]
        for n, (sub, b) in enumerate(items):
            if n + STREAM_AHEAD < len(items):
                pending.append(scores(items[n + STREAM_AHEAD]))
            if b == 0:
                qn2 = jnp.concatenate([qn_ref[0, h, sub] for h in range(Q_PER_KV)], axis=1)
                bound = jnp.sqrt(qn2 * k2_max) * BOUND_MARGIN
                acc = jnp.zeros((HEAD_DIM, lanes), F32)
                den = jnp.zeros((SUBLANES, lanes), F32)
            p = jnp.exp2(pending.pop(0) - bound)
            den = den + jnp.sum(p.reshape(STREAM_KEYS // SUBLANES, SUBLANES, lanes), axis=0)
            j, part = divmod(b, per_tile)
            vt = vt_ref[0, 0, j, 0:HEAD_DIM, part * STREAM_KEYS:(part + 1) * STREAM_KEYS]
            acc = acc + jnp.dot(vt, p.astype(BF16), preferred_element_type=F32)
            if b == n_blk - 1:
                finish(sub, acc, jnp.sum(den, axis=0, keepdims=True))

    def online_max():
        def one_sub_tile(sub, carry):
            qt = queries(sub)
            m_ref[...] = jnp.full(m_ref.shape, -1e30, F32)
            acc_ref[...] = jnp.zeros(acc_ref.shape, F32)

            def scores(j):
                s_ref[j % 2] = jnp.dot(k_ref[0, j * KEY_TILE:(j + 1) * KEY_TILE, :], qt,
                                       preferred_element_type=F32)

            def softmax(j):
                s = s_ref[j % 2]
                m_old = m_ref[...]
                m_new = jnp.maximum(m_old, jnp.max(s, axis=0, keepdims=True))
                al_ref[j % 2] = jnp.exp2(m_old - m_new)
                m_ref[...] = m_new
                p_ref[j % 2] = jnp.exp2(s - m_new).astype(BF16)

            def accumulate(j):
                pv = jnp.dot(vt_ref[0, 0, j], p_ref[j % 2], preferred_element_type=F32)
                acc_ref[...] = al_ref[j % 2] * acc_ref[...] + pv

            for t in range(n_kt + 2):
                if t < n_kt:
                    scores(t)
                if 1 <= t <= n_kt:
                    softmax(t - 1)
                if t >= 2:
                    accumulate(t - 2)
            acc = acc_ref[...]
            finish(sub, acc[0:HEAD_DIM], acc[HEAD_DIM:HEAD_DIM + 1])
            return carry

        jax.lax.fori_loop(0, n_sub, one_sub_tile, 0)

    jax.lax.cond(safe_ref[pl.program_id(0), pl.program_id(1)] != 0, shifted_by_bound, online_max)


def _attention(qt, qn2, q2_tiles, k, k2_tiles, vt):
    B, _, n_qs, _, _ = qt.shape
    S = n_qs * Q_SUB
    n_kt = vt.shape[2]
    n_sub = max(1, min(UNROLLED_PASS_TILES // n_kt, n_qs))
    q_tile = n_sub * Q_SUB
    assert S % q_tile == 0
    lanes = Q_PER_KV * Q_SUB

    q2_max = jnp.max(q2_tiles[:, :, :, 0], axis=1).reshape(B, N_KV_HEADS, Q_PER_KV).max(axis=-1)
    k2_max = jnp.max(k2_tiles[:, :, :N_KV_HEADS, 0], axis=1)
    safe = (jnp.sqrt(q2_max * k2_max) * BOUND_MARGIN <= MAX_BOUND_SHIFT).astype(jnp.int32)
    k2_lanes = jnp.broadcast_to(k2_max[:, :, None, None], (B, N_KV_HEADS, 1, LANES))

    grid_spec = pltpu.PrefetchScalarGridSpec(
        num_scalar_prefetch=1,
        grid=(B, N_KV_HEADS, S // q_tile),
        in_specs=[
            pl.BlockSpec((1, Q_PER_KV, n_sub, HEAD_DIM, Q_SUB),
                         lambda b, g, i, safe: (b, g, i, 0, 0)),
            pl.BlockSpec((1, Q_PER_KV, n_sub, 1, Q_SUB), lambda b, g, i, safe: (b, g, i, 0, 0)),
            pl.BlockSpec((1, S, KV_WIDTH), lambda b, g, i, safe: (b, 0, 0)),
            pl.BlockSpec((1, 1, 1, LANES), lambda b, g, i, safe: (b, g, 0, 0)),
            pl.BlockSpec((1, 1, n_kt, V_ROWS, KEY_TILE), lambda b, g, i, safe: (b, g, 0, 0, 0)),
        ],
        out_specs=pl.BlockSpec((1, q_tile, Q_PER_KV * HEAD_DIM), lambda b, g, i, safe: (b, i, g)),
        scratch_shapes=[
            pltpu.VMEM((2, KEY_TILE, lanes), F32),
            pltpu.VMEM((2, KEY_TILE, lanes), BF16),
            pltpu.VMEM((1, lanes), F32),
            pltpu.VMEM((2, 1, lanes), F32),
            pltpu.VMEM((V_ROWS, lanes), F32),
        ],
    )
    return pl.pallas_call(
        _attn_kernel,
        grid_spec=grid_spec,
        out_shape=jax.ShapeDtypeStruct((B, S, ATTN_WIDTH), BF16),
        compiler_params=pltpu.CompilerParams(
            dimension_semantics=("parallel", "parallel", "arbitrary"),
            vmem_limit_bytes=VMEM_LIMIT_BYTES),
        name="attention",
    )(safe, qt, qn2, k, k2_lanes, vt)


def _mix_mlp_kernel(x_ref, a_ref, u_ref, up_ref, un_ref, wp_ref, ps_ref, wo_ref, g2_ref,
                    wu_ref, wd_ref, y_ref):
    i = pl.program_id(1)
    nt = pl.num_programs(1)
    T = x_ref.shape[1]
    S = T * nt
    n_ext = T + 2 * POOL_HALO

    u = u_ref[0]
    ext = jnp.concatenate([jnp.where(i > 0, up_ref[0], 0.0), u,
                           jnp.where(i < nt - 1, un_ref[0], 0.0)], axis=0)

    def rows_from(v, k):
        return pltpu.roll(v, (-k) % n_ext, 0) if k % n_ext else v

    row = jax.lax.broadcasted_iota(jnp.int32, (POOL_HALO, 1), 0)
    t_first = i * T + row
    t_last = i * T + (T - POOL_HALO) + row

    centred = []
    for g, win in enumerate(POOL_WINDOWS):
        sl = slice(g * POOL_GROUP_DIM, (g + 1) * POOL_GROUP_DIM)
        half = win // 2
        run, length = ext[:, sl], 1
        while length < win:
            run = run + rows_from(run, -length)
            length *= 2
        wsum = rows_from(run, half - 1)[POOL_HALO:POOL_HALO + T]

        def clipped(tok):
            return (jnp.minimum(tok + half, S) - jnp.maximum(tok - half, 0)).astype(F32)

        ug = u[:, sl]
        inner = slice(POOL_HALO, T - POOL_HALO)
        centred.append(jnp.concatenate([
            wsum[:POOL_HALO] / clipped(t_first) - ug[:POOL_HALO],
            wsum[inner] * (1.0 / win) - ug[inner],
            wsum[T - POOL_HALO:] / clipped(t_last) - ug[T - POOL_HALO:]], axis=0).astype(BF16))

    rows_per = T // MLP_ROW_BLOCKS
    residual, normed = [], []
    for r in range(MLP_ROW_BLOCKS):
        rows = slice(r * rows_per, (r + 1) * rows_per)
        mixed = [a_ref[0, rows, :]]
        for pair in range(N_POOL_GROUPS // POOL_PAIR):
            d = jnp.concatenate([c[rows] for c in centred[pair * POOL_PAIR:(pair + 1) * POOL_PAIR]],
                                axis=1)
            z = jnp.dot(d, wp_ref[pair], preferred_element_type=F32)
            z = z * ps_ref[:, pair * MXU_DIM:(pair + 1) * MXU_DIM]
            mixed.append(z.astype(BF16))
        mix = jnp.dot(jnp.concatenate(mixed, axis=1), wo_ref[...], preferred_element_type=F32)
        x1 = x_ref[0, rows, :] + mix
        ms = jnp.mean(x1 * x1, axis=-1, keepdims=True)
        residual.append(x1)
        normed.append((x1 * jax.lax.rsqrt(ms + EPS) * g2_ref[...]).astype(BF16))

    for r in range(MLP_ROW_BLOCKS):
        acc = residual[r]
        for c in range(D_FF // FF_CHUNK):
            sl = slice(c * FF_CHUNK, (c + 1) * FF_CHUNK)
            f = jnp.dot(normed[r], wu_ref[:, sl], preferred_element_type=F32)
            f = jnp.square(jnp.maximum(f, 0.0)).astype(BF16)
            acc = acc + jnp.dot(f, wd_ref[sl, :], preferred_element_type=F32)
        y_ref[0, r * rows_per:(r + 1) * rows_per, :] = acc


def _mix_mlp(x, a, u, w_pool_bd, pool_scale, w_out, g2, w_up, w_down):
    B, S, _ = x.shape
    T = SEQ_TILE
    nt = S // T
    hb = T // POOL_HALO
    n_hb = S // POOL_HALO
    const2 = lambda b, i: (0, 0)
    resident = functools.partial(pl.BlockSpec, pipeline_mode=pl.Buffered(1))
    return pl.pallas_call(
        _mix_mlp_kernel,
        grid=(B, nt),
        in_specs=[
            pl.BlockSpec((1, T, D_MODEL), lambda b, i: (b, i, 0)),
            pl.BlockSpec((1, T, ATTN_WIDTH), lambda b, i: (b, i, 0)),
            pl.BlockSpec((1, T, POOL_WIDTH), lambda b, i: (b, i, 0)),
            pl.BlockSpec((1, POOL_HALO, POOL_WIDTH),
                         lambda b, i: (b, jnp.maximum(i * hb - 1, 0), 0)),
            pl.BlockSpec((1, POOL_HALO, POOL_WIDTH),
                         lambda b, i: (b, jnp.minimum((i + 1) * hb, n_hb - 1), 0)),
            resident((N_POOL_GROUPS // POOL_PAIR, MXU_DIM, MXU_DIM), lambda b, i: (0, 0, 0)),
            pl.BlockSpec((1, POOL_WIDTH), const2),
            resident((D_MODEL, D_MODEL), const2),
            pl.BlockSpec((1, D_MODEL), const2),
            resident((D_MODEL, D_FF), const2),
            resident((D_FF, D_MODEL), const2),
        ],
        out_specs=pl.BlockSpec((1, T, D_MODEL), lambda b, i: (b, i, 0)),
        out_shape=jax.ShapeDtypeStruct((B, S, D_MODEL), F32),
        compiler_params=pltpu.CompilerParams(
            dimension_semantics=("parallel", "parallel"),
            vmem_limit_bytes=VMEM_LIMIT_BYTES),
        name="mix_mlp",
    )(x, a, u, u, u, w_pool_bd, pool_scale, w_out, g2, w_up, w_down)


def _layer(x, g1, w_u, w_tr, qg, kg, w_pool_bd, pool_scale, w_out, g2, w_up, w_down, tables):
    qt, qn2, q2_tiles, k, k2_tiles, vt, u = _in_proj(x, g1, w_u, w_tr, qg, kg, tables)
    a = _attention(qt, qn2, q2_tiles, k, k2_tiles, vt)
    return _mix_mlp(x, a, u, w_pool_bd, pool_scale, w_out, g2, w_up, w_down)


def _trunk(x, layers):
    tables = _rope_tables(x.shape[1])
    for params in layers:
        x = _layer(x, *params, tables)
    return x


def _pool_block_diagonal(w_pool):
    c = POOL_GROUP_DIM
    out = jnp.zeros((N_POOL_GROUPS // POOL_PAIR, MXU_DIM, MXU_DIM), w_pool.dtype)
    for g in range(N_POOL_GROUPS):
        pair, slot = divmod(g, POOL_PAIR)
        out = out.at[pair, slot * c:(slot + 1) * c, slot * c:(slot + 1) * c].set(w_pool[g])
    return out


def kernel(x_prompt, x_sample, norm1_g, w_in, q_norm_g, k_norm_g, w_pool, pool_scale, w_out,
           norm2_g, w_up, w_down):
    depth = w_in.shape[0]
    qkv_end = ATTN_WIDTH + 2 * KV_WIDTH
    layers = []
    for l in range(depth):
        w = w_in[l].astype(BF16)
        layers.append((
            norm1_g[l][None, :],
            w[:, qkv_end:],
            w[:, :qkv_end].T,
            q_norm_g[l][:, None],
            k_norm_g[l][:, None],
            _pool_block_diagonal(w_pool[l].astype(BF16)),
            pool_scale[l][None, :],
            w_out[l].astype(BF16),
            norm2_g[l][None, :],
            w_up[l].astype(BF16),
            w_down[l].astype(BF16),
        ))
    return (_trunk(x_prompt, layers), _trunk(x_sample, layers))
```

```python
import functools
import math

import jax
import jax.numpy as jnp
from jax.experimental import pallas as pl
from jax.experimental.pallas import tpu as pltpu

D_MODEL = 1024
GRID_W = 64
HEAD_DIM = 64
N_Q_HEADS = 8
N_KV_HEADS = 2
Q_PER_KV = N_Q_HEADS // N_KV_HEADS
ATTN_WIDTH = N_Q_HEADS * HEAD_DIM
KV_WIDTH = N_KV_HEADS * HEAD_DIM
POOL_WINDOWS = (2, 4, 8, 16)
N_POOL_GROUPS = len(POOL_WINDOWS)
POOL_WIDTH = D_MODEL - ATTN_WIDTH
POOL_GROUP_DIM = POOL_WIDTH // N_POOL_GROUPS
D_FF = 4 * D_MODEL
ROPE_THETA = 10000.0
EPS = 1e-6
N_FREQ = HEAD_DIM // 4

LANES = 128
SUBLANES = 8
BF16_ROWS = 16
MXU_DIM = 256
MXU_ROWS = 256
POOL_HALO = 8
assert all(w & (w - 1) == 0 and w // 2 <= POOL_HALO for w in POOL_WINDOWS)
POOL_PAIR = MXU_DIM // POOL_GROUP_DIM
assert N_POOL_GROUPS % POOL_PAIR == 0
VMEM_LIMIT_BYTES = 56 * 1024 * 1024

PROJ_TILE = 1024
SEQ_TILE = 1024
KEY_TILE = 512
Q_SUB = 128
UNROLLED_PASS_TILES = 64
FF_CHUNK = 1024
MLP_ROW_BLOCKS = 4
V_ROWS = HEAD_DIM + BF16_ROWS
MAX_BOUND_SHIFT = 30.0
BOUND_MARGIN = 1.02
Q_SCALE = HEAD_DIM ** -0.5 * math.log2(math.e)
STREAM_KEYS = 256
STREAM_AHEAD = 2

F32 = jnp.float32
BF16 = jnp.bfloat16

TR_WIDTH = ATTN_WIDTH + 2 * KV_WIDTH


def _rope_tables(seq_len):
    rows = seq_len // GRID_W
    row_ids = jnp.broadcast_to(jnp.arange(rows, dtype=F32)[:, None], (rows, GRID_W)).reshape(-1)
    col_ids = jnp.broadcast_to(jnp.arange(GRID_W, dtype=F32)[None, :], (rows, GRID_W)).reshape(-1)
    inv_freq = ROPE_THETA ** (-jnp.arange(N_FREQ, dtype=F32) / N_FREQ)
    ang = jnp.stack([row_ids[:, None] * inv_freq, col_ids[:, None] * inv_freq], axis=1)
    cos, sin = jnp.cos(ang), jnp.sin(ang)
    cos_h = jnp.stack([cos, cos], axis=2).reshape(seq_len, HEAD_DIM)
    sin_sg = jnp.stack([-sin, sin], axis=2).reshape(seq_len, HEAD_DIM)
    return cos_h.T, sin_sg.T


def _in_proj_kernel(x_ref, g1_ref, w_u_ref, w_tr_ref, qg_ref, kg_ref, cos_t_ref, sin_t_ref,
                    qt_ref, k_ref, vt_ref, u_ref):
    x = x_ref[0]
    ms = jnp.mean(x * x, axis=-1, keepdims=True)
    h = (x * jax.lax.rsqrt(ms + EPS) * g1_ref[...]).astype(BF16)
    n_tok = h.shape[0]
    for r in range(0, n_tok, MXU_ROWS):
        u_ref[0, r:r + MXU_ROWS, :] = jnp.dot(h[r:r + MXU_ROWS], w_u_ref[...],
                                              preferred_element_type=F32)
    tr = jnp.concatenate(
        [jax.lax.dot_general(w_tr_ref[r:r + MXU_ROWS, :], h, (((1,), (1,)), ((), ())),
                             preferred_element_type=F32)
         for r in range(0, TR_WIDTH, MXU_ROWS)], axis=0)
    cos_t, sin_t = cos_t_ref[...], sin_t_ref[...]

    def norm_rope(t, gain):
        tn = t * jax.lax.rsqrt(jnp.mean(t * t, axis=0, keepdims=True) + EPS) * gain
        partner = jnp.concatenate([tn[N_FREQ:2 * N_FREQ], tn[0:N_FREQ],
                                   tn[3 * N_FREQ:4 * N_FREQ], tn[2 * N_FREQ:3 * N_FREQ]], axis=0)
        return tn * cos_t + partner * sin_t

    q_gain = qg_ref[...] * Q_SCALE
    for hd in range(N_Q_HEADS):
        qr = norm_rope(tr[hd * HEAD_DIM:(hd + 1) * HEAD_DIM, :], q_gain).astype(BF16)
        for c in range(n_tok // Q_SUB):
            qt_ref[0, hd, c] = qr[:, c * Q_SUB:(c + 1) * Q_SUB]

    k_gain = kg_ref[...]
    kr_heads = [norm_rope(tr[ATTN_WIDTH + g * HEAD_DIM:ATTN_WIDTH + (g + 1) * HEAD_DIM, :], k_gain)
                for g in range(N_KV_HEADS)]
    k_tok = jnp.concatenate(kr_heads, axis=0).T
    for g in range(N_KV_HEADS):
        k_ref[0, g] = k_tok[:, g * HEAD_DIM:(g + 1) * HEAD_DIM].astype(BF16)

    row = jax.lax.broadcasted_iota(jnp.int32, (BF16_ROWS, KEY_TILE), 0)
    ones_rows = jnp.where(row == 0, 1.0, 0.0).astype(BF16)
    for g in range(N_KV_HEADS):
        lo = ATTN_WIDTH + KV_WIDTH + g * HEAD_DIM
        vg = tr[lo:lo + HEAD_DIM, :].astype(BF16)
        for c in range(n_tok // KEY_TILE):
            vt_ref[0, g, c, 0:HEAD_DIM, :] = vg[:, c * KEY_TILE:(c + 1) * KEY_TILE]
            vt_ref[0, g, c, HEAD_DIM:, :] = ones_rows


def _in_proj(x, g1, w_u, w_tr, qg, kg, tables):
    B, S, _ = x.shape
    T = PROJ_TILE
    assert S % T == 0 and T % KEY_TILE == 0 and T % Q_SUB == 0
    nt = S // T
    const = lambda b, i: (0, 0)
    cos_t, sin_t = tables
    return pl.pallas_call(
        _in_proj_kernel,
        grid=(B, nt),
        in_specs=[
            pl.BlockSpec((1, T, D_MODEL), lambda b, i: (b, i, 0)),
            pl.BlockSpec((1, D_MODEL), const),
            pl.BlockSpec((D_MODEL, POOL_WIDTH), const),
            pl.BlockSpec((TR_WIDTH, D_MODEL), const),
            pl.BlockSpec((HEAD_DIM, 1), const),
            pl.BlockSpec((HEAD_DIM, 1), const),
            pl.BlockSpec((HEAD_DIM, T), lambda b, i: (0, i)),
            pl.BlockSpec((HEAD_DIM, T), lambda b, i: (0, i)),
        ],
        out_specs=[
            pl.BlockSpec((1, N_Q_HEADS, T // Q_SUB, HEAD_DIM, Q_SUB), lambda b, i: (b, 0, i, 0, 0)),
            pl.BlockSpec((1, N_KV_HEADS, T, HEAD_DIM), lambda b, i: (b, 0, i, 0)),
            pl.BlockSpec((1, N_KV_HEADS, T // KEY_TILE, V_ROWS, KEY_TILE),
                         lambda b, i: (b, 0, i, 0, 0)),
            pl.BlockSpec((1, T, POOL_WIDTH), lambda b, i: (b, i, 0)),
        ],
        out_shape=[
            jax.ShapeDtypeStruct((B, N_Q_HEADS, S // Q_SUB, HEAD_DIM, Q_SUB), BF16),
            jax.ShapeDtypeStruct((B, N_KV_HEADS, S, HEAD_DIM), BF16),
            jax.ShapeDtypeStruct((B, N_KV_HEADS, S // KEY_TILE, V_ROWS, KEY_TILE), BF16),
            jax.ShapeDtypeStruct((B, S, POOL_WIDTH), F32),
        ],
        compiler_params=pltpu.CompilerParams(
            dimension_semantics=("parallel", "parallel"),
            vmem_limit_bytes=VMEM_LIMIT_BYTES),
        name="in_proj",
    )(x, g1, w_u, w_tr, qg, kg, cos_t, sin_t)


def _attn_kernel(safe_ref, bound_ref, qt_ref, k_ref, vt_ref, o_ref,
                 s_ref, p_ref, m_ref, al_ref, acc_ref):
    n_kt = vt_ref.shape[2]
    n_sub = qt_ref.shape[2]
    lanes = Q_PER_KV * Q_SUB

    def queries(sub):
        return jnp.concatenate([qt_ref[0, h, sub] for h in range(Q_PER_KV)], axis=1)

    def finish(sub, acc, den):
        out_t = acc / den
        pairs = [jnp.concatenate([out_t[:, h * Q_SUB:(h + 1) * Q_SUB] for h in (p, p + 1)], axis=0).T
                 for p in range(0, Q_PER_KV, 2)]
        out = jnp.concatenate(pairs, axis=1)
        o_ref[0, pl.ds(pl.multiple_of(sub * Q_SUB, Q_SUB), Q_SUB), :] = out.astype(o_ref.dtype)

    def shifted_by_bound():
        bound = bound_ref[0]
        per_tile = KEY_TILE // STREAM_KEYS
        n_blk = n_kt * per_tile
        qts = {}

        def scores(item):
            sub, b = item
            if sub not in qts:
                qts[sub] = queries(sub)
            return jnp.dot(k_ref[0, 0, b * STREAM_KEYS:(b + 1) * STREAM_KEYS, :], qts[sub],
                           preferred_element_type=F32)

        items = [(sub, b) for sub in range(n_sub) for b in range(n_blk)]
        pending = [scores(it) for it in items[:STREAM_AHEAD]]
        for n, (sub, b) in enumerate(items):
            if n + STREAM_AHEAD < len(items):
                pending.append(scores(items[n + STREAM_AHEAD]))
            if b == 0:
                acc = jnp.zeros((HEAD_DIM, lanes), F32)
                den = jnp.zeros((SUBLANES, lanes), F32)
            p = jnp.exp2(pending.pop(0) - bound)
            den = den + jnp.sum(p.reshape(STREAM_KEYS // SUBLANES, SUBLANES, lanes), axis=0)
            j, part = divmod(b, per_tile)
            vt = vt_ref[0, 0, j, 0:HEAD_DIM, part * STREAM_KEYS:(part + 1) * STREAM_KEYS]
            acc = acc + jnp.dot(vt, p.astype(BF16), preferred_element_type=F32)
            if b == n_blk - 1:
                finish(sub, acc, jnp.sum(den, axis=0, keepdims=True))

    def online_max():
        def one_sub_tile(sub, carry):
            qt = queries(sub)
            m_ref[...] = jnp.full(m_ref.shape, -1e30, F32)
            acc_ref[...] = jnp.zeros(acc_ref.shape, F32)

            def scores(j):
                s_ref[j % 2] = jnp.dot(k_ref[0, 0, j * KEY_TILE:(j + 1) * KEY_TILE, :], qt,
                                       preferred_element_type=F32)

            def softmax(j):
                s = s_ref[j % 2]
                m_old = m_ref[...]
                m_new = jnp.maximum(m_old, jnp.max(s, axis=0, keepdims=True))
                al_ref[j % 2] = jnp.exp2(m_old - m_new)
                m_ref[...] = m_new
                p_ref[j % 2] = jnp.exp2(s - m_new).astype(BF16)

            def accumulate(j):
                pv = jnp.dot(vt_ref[0, 0, j], p_ref[j % 2], preferred_element_type=F32)
                acc_ref[...] = al_ref[j % 2] * acc_ref[...] + pv

            for t in range(n_kt + 2):
                if t < n_kt:
                    scores(t)
                if 1 <= t <= n_kt:
                    softmax(t - 1)
                if t >= 2:
                    accumulate(t - 2)
            acc = acc_ref[...]
            finish(sub, acc[0:HEAD_DIM], acc[HEAD_DIM:HEAD_DIM + 1])
            return carry

        jax.lax.fori_loop(0, n_sub, one_sub_tile, 0)

    jax.lax.cond(safe_ref[0] != 0, shifted_by_bound, online_max)


def _score_bound(q_gain, k_gain):
    return (HEAD_DIM * Q_SCALE * BOUND_MARGIN) * jnp.max(jnp.abs(q_gain)) * jnp.max(jnp.abs(k_gain))


def _attention(qt, k, vt, bound):
    B, _, n_qs, _, _ = qt.shape
    S = n_qs * Q_SUB
    n_kt = vt.shape[2]
    n_sub = max(1, min(UNROLLED_PASS_TILES // n_kt, n_qs))
    q_tile = n_sub * Q_SUB
    assert S % q_tile == 0
    lanes = Q_PER_KV * Q_SUB

    safe = (bound <= MAX_BOUND_SHIFT).astype(jnp.int32).reshape(1)
    bound = bound.astype(F32).reshape(1)

    grid_spec = pltpu.PrefetchScalarGridSpec(
        num_scalar_prefetch=2,
        grid=(B, N_KV_HEADS, S // q_tile),
        in_specs=[
            pl.BlockSpec((1, Q_PER_KV, n_sub, HEAD_DIM, Q_SUB), lambda b, g, i, *_: (b, g, i, 0, 0)),
            pl.BlockSpec((1, 1, S, HEAD_DIM), lambda b, g, i, *_: (b, g, 0, 0)),
            pl.BlockSpec((1, 1, n_kt, V_ROWS, KEY_TILE), lambda b, g, i, *_: (b, g, 0, 0, 0)),
        ],
        out_specs=pl.BlockSpec((1, q_tile, Q_PER_KV * HEAD_DIM), lambda b, g, i, *_: (b, i, g)),
        scratch_shapes=[
            pltpu.VMEM((2, KEY_TILE, lanes), F32),
            pltpu.VMEM((2, KEY_TILE, lanes), BF16),
            pltpu.VMEM((1, lanes), F32),
            pltpu.VMEM((2, 1, lanes), F32),
            pltpu.VMEM((V_ROWS, lanes), F32),
        ],
    )
    return pl.pallas_call(
        _attn_kernel,
        grid_spec=grid_spec,
        out_shape=jax.ShapeDtypeStruct((B, S, ATTN_WIDTH), BF16),
        compiler_params=pltpu.CompilerParams(
            dimension_semantics=("parallel", "parallel", "arbitrary"),
            vmem_limit_bytes=VMEM_LIMIT_BYTES),
        name="attention",
    )(safe, bound, qt, k, vt)


def _mix_mlp_kernel(x_ref, a_ref, u_ref, up_ref, un_ref, wp_ref, ps_ref, wo_ref, g2_ref,
                    wu_ref, wd_ref, y_ref):
    i = pl.program_id(1)
    nt = pl.num_programs(1)
    T = x_ref.shape[1]
    S = T * nt
    n_ext = T + 2 * POOL_HALO

    u = u_ref[0]
    ext = jnp.concatenate([jnp.where(i > 0, up_ref[0], 0.0), u,
                           jnp.where(i < nt - 1, un_ref[0], 0.0)], axis=0)

    def rows_from(v, k):
        return pltpu.roll(v, (-k) % n_ext, 0) if k % n_ext else v

    row = jax.lax.broadcasted_iota(jnp.int32, (POOL_HALO, 1), 0)
    t_first = i * T + row
    t_last = i * T + (T - POOL_HALO) + row

    centred = []
    for g, win in enumerate(POOL_WINDOWS):
        sl = slice(g * POOL_GROUP_DIM, (g + 1) * POOL_GROUP_DIM)
        half = win // 2
        run, length = ext[:, sl], 1
        while length < win:
            run = run + rows_from(run, -length)
            length *= 2
        wsum = rows_from(run, half - 1)[POOL_HALO:POOL_HALO + T]

        def clipped(tok):
            return (jnp.minimum(tok + half, S) - jnp.maximum(tok - half, 0)).astype(F32)

        ug = u[:, sl]
        inner = slice(POOL_HALO, T - POOL_HALO)
        centred.append(jnp.concatenate([
            wsum[:POOL_HALO] / clipped(t_first) - ug[:POOL_HALO],
            wsum[inner] * (1.0 / win) - ug[inner],
            wsum[T - POOL_HALO:] / clipped(t_last) - ug[T - POOL_HALO:]], axis=0).astype(BF16))

    rows_per = T // MLP_ROW_BLOCKS
    residual, normed = [], []
    for r in range(MLP_ROW_BLOCKS):
        rows = slice(r * rows_per, (r + 1) * rows_per)
        mixed = [a_ref[0, rows, :]]
        for pair in range(N_POOL_GROUPS // POOL_PAIR):
            d = jnp.concatenate([c[rows] for c in centred[pair * POOL_PAIR:(pair + 1) * POOL_PAIR]],
                                axis=1)
            z = jnp.dot(d, wp_ref[pair], preferred_element_type=F32)
            z = z * ps_ref[:, pair * MXU_DIM:(pair + 1) * MXU_DIM]
            mixed.append(z.astype(BF16))
        mix = jnp.dot(jnp.concatenate(mixed, axis=1), wo_ref[...], preferred_element_type=F32)
        x1 = x_ref[0, rows, :] + mix
        ms = jnp.mean(x1 * x1, axis=-1, keepdims=True)
        residual.append(x1)
        normed.append((x1 * jax.lax.rsqrt(ms + EPS) * g2_ref[...]).astype(BF16))

    for r in range(MLP_ROW_BLOCKS):
        acc = residual[r]
        for c in range(D_FF // FF_CHUNK):
            sl = slice(c * FF_CHUNK, (c + 1) * FF_CHUNK)
            f = jnp.dot(normed[r], wu_ref[:, sl], preferred_element_type=F32)
            f = jnp.square(jnp.maximum(f, 0.0)).astype(BF16)
            acc = acc + jnp.dot(f, wd_ref[sl, :], preferred_element_type=F32)
        y_ref[0, r * rows_per:(r + 1) * rows_per, :] = acc


def _mix_mlp(x, a, u, w_pool_bd, pool_scale, w_out, g2, w_up, w_down):
    B, S, _ = x.shape
    T = SEQ_TILE
    nt = S // T
    hb = T // POOL_HALO
    n_hb = S // POOL_HALO
    const2 = lambda b, i: (0, 0)
    resident = functools.partial(pl.BlockSpec, pipeline_mode=pl.Buffered(1))
    return pl.pallas_call(
        _mix_mlp_kernel,
        grid=(B, nt),
        in_specs=[
            pl.BlockSpec((1, T, D_MODEL), lambda b, i: (b, i, 0)),
            pl.BlockSpec((1, T, ATTN_WIDTH), lambda b, i: (b, i, 0)),
            pl.BlockSpec((1, T, POOL_WIDTH), lambda b, i: (b, i, 0)),
            pl.BlockSpec((1, POOL_HALO, POOL_WIDTH),
                         lambda b, i: (b, jnp.maximum(i * hb - 1, 0), 0)),
            pl.BlockSpec((1, POOL_HALO, POOL_WIDTH),
                         lambda b, i: (b, jnp.minimum((i + 1) * hb, n_hb - 1), 0)),
            resident((N_POOL_GROUPS // POOL_PAIR, MXU_DIM, MXU_DIM), lambda b, i: (0, 0, 0)),
            pl.BlockSpec((1, POOL_WIDTH), const2),
            resident((D_MODEL, D_MODEL), const2),
            pl.BlockSpec((1, D_MODEL), const2),
            resident((D_MODEL, D_FF), const2),
            resident((D_FF, D_MODEL), const2),
        ],
        out_specs=pl.BlockSpec((1, T, D_MODEL), lambda b, i: (b, i, 0)),
        out_shape=jax.ShapeDtypeStruct((B, S, D_MODEL), F32),
        compiler_params=pltpu.CompilerParams(
            dimension_semantics=("parallel", "parallel"),
            vmem_limit_bytes=VMEM_LIMIT_BYTES),
        name="mix_mlp",
    )(x, a, u, u, u, w_pool_bd, pool_scale, w_out, g2, w_up, w_down)


def _layer(x, g1, w_u, w_tr, qg, kg, w_pool_bd, pool_scale, w_out, g2, w_up, w_down, tables):
    qt, k, vt, u = _in_proj(x, g1, w_u, w_tr, qg, kg, tables)
    a = _attention(qt, k, vt, _score_bound(qg, kg))
    return _mix_mlp(x, a, u, w_pool_bd, pool_scale, w_out, g2, w_up, w_down)


def _trunk(x, layers):
    tables = _rope_tables(x.shape[1])
    for params in layers:
        x = _layer(x, *params, tables)
    return x


def _pool_block_diagonal(w_pool):
    c = POOL_GROUP_DIM
    out = jnp.zeros((N_POOL_GROUPS // POOL_PAIR, MXU_DIM, MXU_DIM), w_pool.dtype)
    for g in range(N_POOL_GROUPS):
        pair, slot = divmod(g, POOL_PAIR)
        out = out.at[pair, slot * c:(slot + 1) * c, slot * c:(slot + 1) * c].set(w_pool[g])
    return out


def kernel(x_prompt, x_sample, norm1_g, w_in, q_norm_g, k_norm_g, w_pool, pool_scale, w_out,
           norm2_g, w_up, w_down):
    depth = w_in.shape[0]
    qkv_end = ATTN_WIDTH + 2 * KV_WIDTH
    layers = []
    for l in range(depth):
        w = w_in[l].astype(BF16)
        layers.append((
            norm1_g[l][None, :],
            w[:, qkv_end:],
            w[:, :qkv_end].T,
            q_norm_g[l][:, None],
            k_norm_g[l][:, None],
            _pool_block_diagonal(w_pool[l].astype(BF16)),
            pool_scale[l][None, :],
            w_out[l].astype(BF16),
            norm2_g[l][None, :],
            w_up[l].astype(BF16),
            w_down[l].astype(BF16),
        ))
    return (_trunk(x_prompt, layers), _trunk(x_sample, layers))
```

```python
import functools
import math

import jax
import jax.numpy as jnp
from jax.experimental import pallas as pl
from jax.experimental.pallas import tpu as pltpu

D_MODEL = 1024
GRID_W = 64
HEAD_DIM = 64
N_Q_HEADS = 8
N_KV_HEADS = 2
Q_PER_KV = N_Q_HEADS // N_KV_HEADS
ATTN_WIDTH = N_Q_HEADS * HEAD_DIM
KV_WIDTH = N_KV_HEADS * HEAD_DIM
POOL_WINDOWS = (2, 4, 8, 16)
N_POOL_GROUPS = len(POOL_WINDOWS)
POOL_WIDTH = D_MODEL - ATTN_WIDTH
POOL_GROUP_DIM = POOL_WIDTH // N_POOL_GROUPS
D_FF = 4 * D_MODEL
ROPE_THETA = 10000.0
EPS = 1e-6
N_FREQ = HEAD_DIM // 4

LANES = 128
SUBLANES = 8
BF16_ROWS = 16
MXU_DIM = 256
MXU_ROWS = 256
POOL_HALO = 8
assert all(w & (w - 1) == 0 and w // 2 <= POOL_HALO for w in POOL_WINDOWS)
POOL_PAIR = MXU_DIM // POOL_GROUP_DIM
assert N_POOL_GROUPS % POOL_PAIR == 0
VMEM_LIMIT_BYTES = 56 * 1024 * 1024

PROJ_TILE = 1024
SEQ_TILE = 1024
KEY_TILE = 512
Q_SUB = 128
UNROLLED_PASS_TILES = 64
FF_CHUNK = 1024
MLP_ROW_BLOCKS = 4
V_ROWS = HEAD_DIM + BF16_ROWS
MAX_SCORE_BOUND = 30.0
BOUND_MARGIN = 1.02
Q_SCALE = HEAD_DIM ** -0.5 * math.log2(math.e)
STREAM_KEYS = 256
STREAM_AHEAD = 2

F32 = jnp.float32
BF16 = jnp.bfloat16

TR_WIDTH = ATTN_WIDTH + 2 * KV_WIDTH


def _rope_tables(seq_len):
    rows = seq_len // GRID_W
    row_ids = jnp.broadcast_to(jnp.arange(rows, dtype=F32)[:, None], (rows, GRID_W)).reshape(-1)
    col_ids = jnp.broadcast_to(jnp.arange(GRID_W, dtype=F32)[None, :], (rows, GRID_W)).reshape(-1)
    inv_freq = ROPE_THETA ** (-jnp.arange(N_FREQ, dtype=F32) / N_FREQ)
    ang = jnp.stack([row_ids[:, None] * inv_freq, col_ids[:, None] * inv_freq], axis=1)
    cos, sin = jnp.cos(ang), jnp.sin(ang)
    cos_h = jnp.stack([cos, cos], axis=2).reshape(seq_len, HEAD_DIM)
    sin_sg = jnp.stack([-sin, sin], axis=2).reshape(seq_len, HEAD_DIM)
    return cos_h.T, sin_sg.T


def _in_proj_kernel(x_ref, g1_ref, w_u_ref, w_tr_ref, qg_ref, kg_ref, cos_t_ref, sin_t_ref,
                    qt_ref, k_ref, vt_ref, u_ref):
    x = x_ref[0]
    ms = jnp.mean(x * x, axis=-1, keepdims=True)
    h = (x * jax.lax.rsqrt(ms + EPS) * g1_ref[...]).astype(BF16)
    n_tok = h.shape[0]
    for r in range(0, n_tok, MXU_ROWS):
        u_ref[0, r:r + MXU_ROWS, :] = jnp.dot(h[r:r + MXU_ROWS], w_u_ref[...],
                                              preferred_element_type=F32)
    tr = jnp.concatenate(
        [jax.lax.dot_general(w_tr_ref[r:r + MXU_ROWS, :], h, (((1,), (1,)), ((), ())),
                             preferred_element_type=F32)
         for r in range(0, TR_WIDTH, MXU_ROWS)], axis=0)
    cos_t, sin_t = cos_t_ref[...], sin_t_ref[...]

    def norm_rope(t, gain):
        tn = t * jax.lax.rsqrt(jnp.mean(t * t, axis=0, keepdims=True) + EPS) * gain
        partner = jnp.concatenate([tn[N_FREQ:2 * N_FREQ], tn[0:N_FREQ],
                                   tn[3 * N_FREQ:4 * N_FREQ], tn[2 * N_FREQ:3 * N_FREQ]], axis=0)
        return tn * cos_t + partner * sin_t

    q_gain = qg_ref[...] * Q_SCALE
    for hd in range(N_Q_HEADS):
        qr = norm_rope(tr[hd * HEAD_DIM:(hd + 1) * HEAD_DIM, :], q_gain).astype(BF16)
        for c in range(n_tok // Q_SUB):
            qt_ref[0, hd, c] = qr[:, c * Q_SUB:(c + 1) * Q_SUB]

    k_gain = kg_ref[...]
    kr_heads = [norm_rope(tr[ATTN_WIDTH + g * HEAD_DIM:ATTN_WIDTH + (g + 1) * HEAD_DIM, :], k_gain)
                for g in range(N_KV_HEADS)]
    k_tok = jnp.concatenate(kr_heads, axis=0).T
    for g in range(N_KV_HEADS):
        k_ref[0, g] = k_tok[:, g * HEAD_DIM:(g + 1) * HEAD_DIM].astype(BF16)

    row = jax.lax.broadcasted_iota(jnp.int32, (BF16_ROWS, KEY_TILE), 0)
    ones_rows = jnp.where(row == 0, 1.0, 0.0).astype(BF16)
    for g in range(N_KV_HEADS):
        lo = ATTN_WIDTH + KV_WIDTH + g * HEAD_DIM
        vg = tr[lo:lo + HEAD_DIM, :].astype(BF16)
        for c in range(n_tok // KEY_TILE):
            vt_ref[0, g, c, 0:HEAD_DIM, :] = vg[:, c * KEY_TILE:(c + 1) * KEY_TILE]
            vt_ref[0, g, c, HEAD_DIM:, :] = ones_rows


def _in_proj(x, g1, w_u, w_tr, qg, kg, tables):
    B, S, _ = x.shape
    T = PROJ_TILE
    assert S % T == 0 and T % KEY_TILE == 0 and T % Q_SUB == 0
    nt = S // T
    const = lambda b, i: (0, 0)
    cos_t, sin_t = tables
    return pl.pallas_call(
        _in_proj_kernel,
        grid=(B, nt),
        in_specs=[
            pl.BlockSpec((1, T, D_MODEL), lambda b, i: (b, i, 0)),
            pl.BlockSpec((1, D_MODEL), const),
            pl.BlockSpec((D_MODEL, POOL_WIDTH), const),
            pl.BlockSpec((TR_WIDTH, D_MODEL), const),
            pl.BlockSpec((HEAD_DIM, 1), const),
            pl.BlockSpec((HEAD_DIM, 1), const),
            pl.BlockSpec((HEAD_DIM, T), lambda b, i: (0, i)),
            pl.BlockSpec((HEAD_DIM, T), lambda b, i: (0, i)),
        ],
        out_specs=[
            pl.BlockSpec((1, N_Q_HEADS, T // Q_SUB, HEAD_DIM, Q_SUB), lambda b, i: (b, 0, i, 0, 0)),
            pl.BlockSpec((1, N_KV_HEADS, T, HEAD_DIM), lambda b, i: (b, 0, i, 0)),
            pl.BlockSpec((1, N_KV_HEADS, T // KEY_TILE, V_ROWS, KEY_TILE),
                         lambda b, i: (b, 0, i, 0, 0)),
            pl.BlockSpec((1, T, POOL_WIDTH), lambda b, i: (b, i, 0)),
        ],
        out_shape=[
            jax.ShapeDtypeStruct((B, N_Q_HEADS, S // Q_SUB, HEAD_DIM, Q_SUB), BF16),
            jax.ShapeDtypeStruct((B, N_KV_HEADS, S, HEAD_DIM), BF16),
            jax.ShapeDtypeStruct((B, N_KV_HEADS, S // KEY_TILE, V_ROWS, KEY_TILE), BF16),
            jax.ShapeDtypeStruct((B, S, POOL_WIDTH), F32),
        ],
        compiler_params=pltpu.CompilerParams(
            dimension_semantics=("parallel", "parallel"),
            vmem_limit_bytes=VMEM_LIMIT_BYTES),
        name="in_proj",
    )(x, g1, w_u, w_tr, qg, kg, cos_t, sin_t)


def _attn_kernel(bounded_ref, qt_ref, k_ref, vt_ref, o_ref,
                 s_ref, p_ref, m_ref, al_ref, acc_ref):
    n_kt = vt_ref.shape[2]
    n_sub = qt_ref.shape[2]
    lanes = Q_PER_KV * Q_SUB

    def queries(sub):
        return jnp.concatenate([qt_ref[0, h, sub] for h in range(Q_PER_KV)], axis=1)

    def finish(sub, acc, den):
        out_t = acc / den
        pairs = [jnp.concatenate([out_t[:, h * Q_SUB:(h + 1) * Q_SUB] for h in (p, p + 1)], axis=0).T
                 for p in range(0, Q_PER_KV, 2)]
        out = jnp.concatenate(pairs, axis=1)
        o_ref[0, pl.ds(pl.multiple_of(sub * Q_SUB, Q_SUB), Q_SUB), :] = out.astype(o_ref.dtype)

    def bounded_scores():
        per_tile = KEY_TILE // STREAM_KEYS
        n_blk = n_kt * per_tile
        qts = {}

        def scores(item):
            sub, b = item
            if sub not in qts:
                qts[sub] = queries(sub)
            return jnp.dot(k_ref[0, 0, b * STREAM_KEYS:(b + 1) * STREAM_KEYS, :], qts[sub],
                           preferred_element_type=F32)

        items = [(sub, b) for sub in range(n_sub) for b in range(n_blk)]
        pending = [scores(it) for it in items[:STREAM_AHEAD]]
        for n, (sub, b) in enumerate(items):
            if n + STREAM_AHEAD < len(items):
                pending.append(scores(items[n + STREAM_AHEAD]))
            if b == 0:
                acc = jnp.zeros((HEAD_DIM, lanes), F32)
                den = jnp.zeros((SUBLANES, lanes), F32)
            p = jnp.exp2(pending.pop(0))
            den = den + jnp.sum(p.reshape(STREAM_KEYS // SUBLANES, SUBLANES, lanes), axis=0)
            j, part = divmod(b, per_tile)
            vt = vt_ref[0, 0, j, 0:HEAD_DIM, part * STREAM_KEYS:(part + 1) * STREAM_KEYS]
            acc = acc + jnp.dot(vt, p.astype(BF16), preferred_element_type=F32)
            if b == n_blk - 1:
                finish(sub, acc, jnp.sum(den, axis=0, keepdims=True))

    def online_max():
        def one_sub_tile(sub, carry):
            qt = queries(sub)
            m_ref[...] = jnp.full(m_ref.shape, -1e30, F32)
            acc_ref[...] = jnp.zeros(acc_ref.shape, F32)

            def scores(j):
                s_ref[j % 2] = jnp.dot(k_ref[0, 0, j * KEY_TILE:(j + 1) * KEY_TILE, :], qt,
                                       preferred_element_type=F32)

            def softmax(j):
                s = s_ref[j % 2]
                m_old = m_ref[...]
                m_new = jnp.maximum(m_old, jnp.max(s, axis=0, keepdims=True))
                al_ref[j % 2] = jnp.exp2(m_old - m_new)
                m_ref[...] = m_new
                p_ref[j % 2] = jnp.exp2(s - m_new).astype(BF16)

            def accumulate(j):
                pv = jnp.dot(vt_ref[0, 0, j], p_ref[j % 2], preferred_element_type=F32)
                acc_ref[...] = al_ref[j % 2] * acc_ref[...] + pv

            for t in range(n_kt + 2):
                if t < n_kt:
                    scores(t)
                if 1 <= t <= n_kt:
                    softmax(t - 1)
                if t >= 2:
                    accumulate(t - 2)
            acc = acc_ref[...]
            finish(sub, acc[0:HEAD_DIM], acc[HEAD_DIM:HEAD_DIM + 1])
            return carry

        jax.lax.fori_loop(0, n_sub, one_sub_tile, 0)

    jax.lax.cond(bounded_ref[0] != 0, bounded_scores, online_max)


def _score_bound(q_gain, k_gain):
    return (HEAD_DIM * Q_SCALE * BOUND_MARGIN) * jnp.max(jnp.abs(q_gain)) * jnp.max(jnp.abs(k_gain))


def _attention(qt, k, vt, bound):
    B, _, n_qs, _, _ = qt.shape
    S = n_qs * Q_SUB
    n_kt = vt.shape[2]
    n_sub = max(1, min(UNROLLED_PASS_TILES // n_kt, n_qs))
    q_tile = n_sub * Q_SUB
    assert S % q_tile == 0
    lanes = Q_PER_KV * Q_SUB

    bounded = (bound <= MAX_SCORE_BOUND).astype(jnp.int32).reshape(1)

    grid_spec = pltpu.PrefetchScalarGridSpec(
        num_scalar_prefetch=1,
        grid=(B, N_KV_HEADS, S // q_tile),
        in_specs=[
            pl.BlockSpec((1, Q_PER_KV, n_sub, HEAD_DIM, Q_SUB), lambda b, g, i, *_: (b, g, i, 0, 0)),
            pl.BlockSpec((1, 1, S, HEAD_DIM), lambda b, g, i, *_: (b, g, 0, 0)),
            pl.BlockSpec((1, 1, n_kt, V_ROWS, KEY_TILE), lambda b, g, i, *_: (b, g, 0, 0, 0)),
        ],
        out_specs=pl.BlockSpec((1, q_tile, Q_PER_KV * HEAD_DIM), lambda b, g, i, *_: (b, i, g)),
        scratch_shapes=[
            pltpu.VMEM((2, KEY_TILE, lanes), F32),
            pltpu.VMEM((2, KEY_TILE, lanes), BF16),
            pltpu.VMEM((1, lanes), F32),
            pltpu.VMEM((2, 1, lanes), F32),
            pltpu.VMEM((V_ROWS, lanes), F32),
        ],
    )
    return pl.pallas_call(
        _attn_kernel,
        grid_spec=grid_spec,
        out_shape=jax.ShapeDtypeStruct((B, S, ATTN_WIDTH), BF16),
        compiler_params=pltpu.CompilerParams(
            dimension_semantics=("parallel", "parallel", "arbitrary"),
            vmem_limit_bytes=VMEM_LIMIT_BYTES),
        name="attention",
    )(bounded, qt, k, vt)


def _mix_mlp_kernel(x_ref, a_ref, u_ref, up_ref, un_ref, wp_ref, ps_ref, wo_ref, g2_ref,
                    wu_ref, wd_ref, y_ref):
    i = pl.program_id(1)
    nt = pl.num_programs(1)
    T = x_ref.shape[1]
    S = T * nt
    n_ext = T + 2 * POOL_HALO

    u = u_ref[0]
    ext = jnp.concatenate([jnp.where(i > 0, up_ref[0], 0.0), u,
                           jnp.where(i < nt - 1, un_ref[0], 0.0)], axis=0)

    def rows_from(v, k):
        return pltpu.roll(v, (-k) % n_ext, 0) if k % n_ext else v

    row = jax.lax.broadcasted_iota(jnp.int32, (POOL_HALO, 1), 0)
    t_first = i * T + row
    t_last = i * T + (T - POOL_HALO) + row

    centred = []
    for g, win in enumerate(POOL_WINDOWS):
        sl = slice(g * POOL_GROUP_DIM, (g + 1) * POOL_GROUP_DIM)
        half = win // 2
        run, length = ext[:, sl], 1
        while length < win:
            run = run + rows_from(run, -length)
            length *= 2
        wsum = rows_from(run, half - 1)[POOL_HALO:POOL_HALO + T]

        def clipped(tok):
            return (jnp.minimum(tok + half, S) - jnp.maximum(tok - half, 0)).astype(F32)

        ug = u[:, sl]
        inner = slice(POOL_HALO, T - POOL_HALO)
        centred.append(jnp.concatenate([
            wsum[:POOL_HALO] / clipped(t_first) - ug[:POOL_HALO],
            wsum[inner] * (1.0 / win) - ug[inner],
            wsum[T - POOL_HALO:] / clipped(t_last) - ug[T - POOL_HALO:]], axis=0).astype(BF16))

    rows_per = T // MLP_ROW_BLOCKS
    residual, normed = [], []
    for r in range(MLP_ROW_BLOCKS):
        rows = slice(r * rows_per, (r + 1) * rows_per)
        mixed = [a_ref[0, rows, :]]
        for pair in range(N_POOL_GROUPS // POOL_PAIR):
            d = jnp.concatenate([c[rows] for c in centred[pair * POOL_PAIR:(pair + 1) * POOL_PAIR]],
                                axis=1)
            z = jnp.dot(d, wp_ref[pair], preferred_element_type=F32)
            z = z * ps_ref[:, pair * MXU_DIM:(pair + 1) * MXU_DIM]
            mixed.append(z.astype(BF16))
        mix = jnp.dot(jnp.concatenate(mixed, axis=1), wo_ref[...], preferred_element_type=F32)
        x1 = x_ref[0, rows, :] + mix
        ms = jnp.mean(x1 * x1, axis=-1, keepdims=True)
        residual.append(x1)
        normed.append((x1 * jax.lax.rsqrt(ms + EPS) * g2_ref[...]).astype(BF16))

    for r in range(MLP_ROW_BLOCKS):
        acc = residual[r]
        for c in range(D_FF // FF_CHUNK):
            sl = slice(c * FF_CHUNK, (c + 1) * FF_CHUNK)
            f = jnp.dot(normed[r], wu_ref[:, sl], preferred_element_type=F32)
            f = jnp.square(jnp.maximum(f, 0.0)).astype(BF16)
            acc = acc + jnp.dot(f, wd_ref[sl, :], preferred_element_type=F32)
        y_ref[0, r * rows_per:(r + 1) * rows_per, :] = acc


def _mix_mlp(x, a, u, w_pool_bd, pool_scale, w_out, g2, w_up, w_down):
    B, S, _ = x.shape
    T = SEQ_TILE
    nt = S // T
    hb = T // POOL_HALO
    n_hb = S // POOL_HALO
    const2 = lambda b, i: (0, 0)
    resident = functools.partial(pl.BlockSpec, pipeline_mode=pl.Buffered(1))
    return pl.pallas_call(
        _mix_mlp_kernel,
        grid=(B, nt),
        in_specs=[
            pl.BlockSpec((1, T, D_MODEL), lambda b, i: (b, i, 0)),
            pl.BlockSpec((1, T, ATTN_WIDTH), lambda b, i: (b, i, 0)),
            pl.BlockSpec((1, T, POOL_WIDTH), lambda b, i: (b, i, 0)),
            pl.BlockSpec((1, POOL_HALO, POOL_WIDTH),
                         lambda b, i: (b, jnp.maximum(i * hb - 1, 0), 0)),
            pl.BlockSpec((1, POOL_HALO, POOL_WIDTH),
                         lambda b, i: (b, jnp.minimum((i + 1) * hb, n_hb - 1), 0)),
            resident((N_POOL_GROUPS // POOL_PAIR, MXU_DIM, MXU_DIM), lambda b, i: (0, 0, 0)),
            pl.BlockSpec((1, POOL_WIDTH), const2),
            resident((D_MODEL, D_MODEL), const2),
            pl.BlockSpec((1, D_MODEL), const2),
            resident((D_MODEL, D_FF), const2),
            resident((D_FF, D_MODEL), const2),
        ],
        out_specs=pl.BlockSpec((1, T, D_MODEL), lambda b, i: (b, i, 0)),
        out_shape=jax.ShapeDtypeStruct((B, S, D_MODEL), F32),
        compiler_params=pltpu.CompilerParams(
            dimension_semantics=("parallel", "parallel"),
            vmem_limit_bytes=VMEM_LIMIT_BYTES),
        name="mix_mlp",
    )(x, a, u, u, u, w_pool_bd, pool_scale, w_out, g2, w_up, w_down)


def _layer(x, g1, w_u, w_tr, qg, kg, w_pool_bd, pool_scale, w_out, g2, w_up, w_down, tables):
    qt, k, vt, u = _in_proj(x, g1, w_u, w_tr, qg, kg, tables)
    a = _attention(qt, k, vt, _score_bound(qg, kg))
    return _mix_mlp(x, a, u, w_pool_bd, pool_scale, w_out, g2, w_up, w_down)


def _trunk(x, layers):
    tables = _rope_tables(x.shape[1])
    for params in layers:
        x = _layer(x, *params, tables)
    return x


def _pool_block_diagonal(w_pool):
    c = POOL_GROUP_DIM
    out = jnp.zeros((N_POOL_GROUPS // POOL_PAIR, MXU_DIM, MXU_DIM), w_pool.dtype)
    for g in range(N_POOL_GROUPS):
        pair, slot = divmod(g, POOL_PAIR)
        out = out.at[pair, slot * c:(slot + 1) * c, slot * c:(slot + 1) * c].set(w_pool[g])
    return out


def kernel(x_prompt, x_sample, norm1_g, w_in, q_norm_g, k_norm_g, w_pool, pool_scale, w_out,
           norm2_g, w_up, w_down):
    depth = w_in.shape[0]
    qkv_end = ATTN_WIDTH + 2 * KV_WIDTH
    layers = []
    for l in range(depth):
        w = w_in[l].astype(BF16)
        layers.append((
            norm1_g[l][None, :],
            w[:, qkv_end:],
            w[:, :qkv_end].T,
            q_norm_g[l][:, None],
            k_norm_g[l][:, None],
            _pool_block_diagonal(w_pool[l].astype(BF16)),
            pool_scale[l][None, :],
            w_out[l].astype(BF16),
            norm2_g[l][None, :],
            w_up[l].astype(BF16),
            w_down[l].astype(BF16),
        ))
    return (_trunk(x_prompt, layers), _trunk(x_sample, layers))
```

```python
import functools
import math

import jax
import jax.numpy as jnp
from jax.experimental import pallas as pl
from jax.experimental.pallas import tpu as pltpu

D_MODEL = 1024
GRID_W = 64
HEAD_DIM = 64
N_Q_HEADS = 8
N_KV_HEADS = 2
Q_PER_KV = N_Q_HEADS // N_KV_HEADS
ATTN_WIDTH = N_Q_HEADS * HEAD_DIM
KV_WIDTH = N_KV_HEADS * HEAD_DIM
POOL_WINDOWS = (2, 4, 8, 16)
N_POOL_GROUPS = len(POOL_WINDOWS)
POOL_WIDTH = D_MODEL - ATTN_WIDTH
POOL_GROUP_DIM = POOL_WIDTH // N_POOL_GROUPS
D_FF = 4 * D_MODEL
ROPE_THETA = 10000.0
EPS = 1e-6
N_FREQ = HEAD_DIM // 4

LANES = 128
SUBLANES = 8
BF16_ROWS = 16
MXU_DIM = 256
MXU_ROWS = 256
POOL_HALO = 8
assert all(w & (w - 1) == 0 and w // 2 <= POOL_HALO for w in POOL_WINDOWS)
POOL_PAIR = MXU_DIM // POOL_GROUP_DIM
assert N_POOL_GROUPS % POOL_PAIR == 0
VMEM_LIMIT_BYTES = 56 * 1024 * 1024

PROJ_TILE = 1024
SEQ_TILE = 1024
KEY_TILE = 512
Q_SUB = 128
UNROLLED_PASS_TILES = 64
FF_CHUNK = 1024
MLP_ROW_BLOCKS = 4
V_ROWS = HEAD_DIM + BF16_ROWS
MAX_SCORE_BOUND = 30.0
BOUND_MARGIN = 1.02
Q_SCALE = HEAD_DIM ** -0.5 * math.log2(math.e)
STREAM_KEYS = 256
STREAM_AHEAD = 2

F32 = jnp.float32
BF16 = jnp.bfloat16

TR_WIDTH = ATTN_WIDTH + 2 * KV_WIDTH


def _rope_tables(seq_len):
    rows = seq_len // GRID_W
    row_ids = jnp.broadcast_to(jnp.arange(rows, dtype=F32)[:, None], (rows, GRID_W)).reshape(-1)
    col_ids = jnp.broadcast_to(jnp.arange(GRID_W, dtype=F32)[None, :], (rows, GRID_W)).reshape(-1)
    inv_freq = ROPE_THETA ** (-jnp.arange(N_FREQ, dtype=F32) / N_FREQ)
    ang = jnp.stack([row_ids[:, None] * inv_freq, col_ids[:, None] * inv_freq], axis=1)
    cos, sin = jnp.cos(ang), jnp.sin(ang)
    cos_h = jnp.stack([cos, cos], axis=2).reshape(seq_len, HEAD_DIM)
    sin_sg = jnp.stack([-sin, sin], axis=2).reshape(seq_len, HEAD_DIM)
    return cos_h.T, sin_sg.T


def _in_proj_kernel(x_ref, g1_ref, w_u_ref, w_tr_ref, qg_ref, kg_ref, cos_t_ref, sin_t_ref,
                    qt_ref, k_ref, vt_ref, u_ref):
    x = x_ref[0]
    ms = jnp.mean(x * x, axis=-1, keepdims=True)
    h = (x * jax.lax.rsqrt(ms + EPS) * g1_ref[...]).astype(BF16)
    n_tok = h.shape[0]
    for r in range(0, n_tok, MXU_ROWS):
        u_ref[0, r:r + MXU_ROWS, :] = jnp.dot(h[r:r + MXU_ROWS], w_u_ref[...],
                                              preferred_element_type=F32)
    tr = jnp.concatenate(
        [jax.lax.dot_general(w_tr_ref[r:r + MXU_ROWS, :], h, (((1,), (1,)), ((), ())),
                             preferred_element_type=F32)
         for r in range(0, TR_WIDTH, MXU_ROWS)], axis=0)
    cos_t, sin_t = cos_t_ref[...], sin_t_ref[...]

    def norm_rope(t, gain):
        tn = t * jax.lax.rsqrt(jnp.mean(t * t, axis=0, keepdims=True) + EPS) * gain
        partner = jnp.concatenate([tn[N_FREQ:2 * N_FREQ], tn[0:N_FREQ],
                                   tn[3 * N_FREQ:4 * N_FREQ], tn[2 * N_FREQ:3 * N_FREQ]], axis=0)
        return tn * cos_t + partner * sin_t

    q_gain = qg_ref[...] * Q_SCALE
    for hd in range(N_Q_HEADS):
        qr = norm_rope(tr[hd * HEAD_DIM:(hd + 1) * HEAD_DIM, :], q_gain).astype(BF16)
        for c in range(n_tok // Q_SUB):
            qt_ref[0, hd, c] = qr[:, c * Q_SUB:(c + 1) * Q_SUB]

    k_gain = kg_ref[...]
    kr_heads = [norm_rope(tr[ATTN_WIDTH + g * HEAD_DIM:ATTN_WIDTH + (g + 1) * HEAD_DIM, :], k_gain)
                for g in range(N_KV_HEADS)]
    k_tok = jnp.concatenate(kr_heads, axis=0).T
    for g in range(N_KV_HEADS):
        k_ref[0, g] = k_tok[:, g * HEAD_DIM:(g + 1) * HEAD_DIM].astype(BF16)

    row = jax.lax.broadcasted_iota(jnp.int32, (BF16_ROWS, KEY_TILE), 0)
    ones_rows = jnp.where(row == 0, 1.0, 0.0).astype(BF16)
    for g in range(N_KV_HEADS):
        lo = ATTN_WIDTH + KV_WIDTH + g * HEAD_DIM
        vg = tr[lo:lo + HEAD_DIM, :].astype(BF16)
        for c in range(n_tok // KEY_TILE):
            vt_ref[0, g, c, 0:HEAD_DIM, :] = vg[:, c * KEY_TILE:(c + 1) * KEY_TILE]
            vt_ref[0, g, c, HEAD_DIM:, :] = ones_rows


def _in_proj(x, g1, w_u, w_tr, qg, kg, tables):
    B, S, _ = x.shape
    T = PROJ_TILE
    assert S % T == 0 and T % KEY_TILE == 0 and T % Q_SUB == 0
    nt = S // T
    const = lambda b, i: (0, 0)
    cos_t, sin_t = tables
    return pl.pallas_call(
        _in_proj_kernel,
        grid=(B, nt),
        in_specs=[
            pl.BlockSpec((1, T, D_MODEL), lambda b, i: (b, i, 0)),
            pl.BlockSpec((1, D_MODEL), const),
            pl.BlockSpec((D_MODEL, POOL_WIDTH), const),
            pl.BlockSpec((TR_WIDTH, D_MODEL), const),
            pl.BlockSpec((HEAD_DIM, 1), const),
            pl.BlockSpec((HEAD_DIM, 1), const),
            pl.BlockSpec((HEAD_DIM, T), lambda b, i: (0, i)),
            pl.BlockSpec((HEAD_DIM, T), lambda b, i: (0, i)),
        ],
        out_specs=[
            pl.BlockSpec((1, N_Q_HEADS, T // Q_SUB, HEAD_DIM, Q_SUB), lambda b, i: (b, 0, i, 0, 0)),
            pl.BlockSpec((1, N_KV_HEADS, T, HEAD_DIM), lambda b, i: (b, 0, i, 0)),
            pl.BlockSpec((1, N_KV_HEADS, T // KEY_TILE, V_ROWS, KEY_TILE),
                         lambda b, i: (b, 0, i, 0, 0)),
            pl.BlockSpec((1, T, POOL_WIDTH), lambda b, i: (b, i, 0)),
        ],
        out_shape=[
            jax.ShapeDtypeStruct((B, N_Q_HEADS, S // Q_SUB, HEAD_DIM, Q_SUB), BF16),
            jax.ShapeDtypeStruct((B, N_KV_HEADS, S, HEAD_DIM), BF16),
            jax.ShapeDtypeStruct((B, N_KV_HEADS, S // KEY_TILE, V_ROWS, KEY_TILE), BF16),
            jax.ShapeDtypeStruct((B, S, POOL_WIDTH), F32),
        ],
        compiler_params=pltpu.CompilerParams(
            dimension_semantics=("parallel", "parallel"),
            vmem_limit_bytes=VMEM_LIMIT_BYTES),
        name="in_proj",
    )(x, g1, w_u, w_tr, qg, kg, cos_t, sin_t)


def _attn_kernel(bounded_ref, qt_ref, k_ref, vt_ref, *rest):
    n_cast = (len(rest) - 6) // 2
    o_ref = rest[n_cast]
    s_ref, p_ref, m_ref, al_ref, acc_ref = rest[2 * n_cast + 1:]
    for src_ref, dst_ref in zip(rest[:n_cast], rest[n_cast + 1:2 * n_cast + 1]):
        dst_ref[...] = src_ref[...].astype(BF16)

    n_kt = vt_ref.shape[2]
    n_sub = qt_ref.shape[2]
    lanes = Q_PER_KV * Q_SUB

    def queries(sub):
        return jnp.concatenate([qt_ref[0, h, sub] for h in range(Q_PER_KV)], axis=1)

    def finish(sub, acc, den):
        out_t = acc / den
        pairs = [jnp.concatenate([out_t[:, h * Q_SUB:(h + 1) * Q_SUB] for h in (p, p + 1)], axis=0).T
                 for p in range(0, Q_PER_KV, 2)]
        out = jnp.concatenate(pairs, axis=1)
        o_ref[0, pl.ds(pl.multiple_of(sub * Q_SUB, Q_SUB), Q_SUB), :] = out.astype(o_ref.dtype)

    def bounded_scores():
        per_tile = KEY_TILE // STREAM_KEYS
        n_blk = n_kt * per_tile
        qts = {}

        def scores(item):
            sub, b = item
            if sub not in qts:
                qts[sub] = queries(sub)
            return jnp.dot(k_ref[0, 0, b * STREAM_KEYS:(b + 1) * STREAM_KEYS, :], qts[sub],
                           preferred_element_type=F32)

        items = [(sub, b) for sub in range(n_sub) for b in range(n_blk)]
        pending = [scores(it) for it in items[:STREAM_AHEAD]]
        for n, (sub, b) in enumerate(items):
            if n + STREAM_AHEAD < len(items):
                pending.append(scores(items[n + STREAM_AHEAD]))
            if b == 0:
                acc = jnp.zeros((HEAD_DIM, lanes), F32)
                den = jnp.zeros((SUBLANES, lanes), F32)
            p = jnp.exp2(pending.pop(0))
            den = den + jnp.sum(p.reshape(STREAM_KEYS // SUBLANES, SUBLANES, lanes), axis=0)
            j, part = divmod(b, per_tile)
            vt = vt_ref[0, 0, j, 0:HEAD_DIM, part * STREAM_KEYS:(part + 1) * STREAM_KEYS]
            acc = acc + jnp.dot(vt, p.astype(BF16), preferred_element_type=F32)
            if b == n_blk - 1:
                finish(sub, acc, jnp.sum(den, axis=0, keepdims=True))

    def online_max():
        def one_sub_tile(sub, carry):
            qt = queries(sub)
            m_ref[...] = jnp.full(m_ref.shape, -1e30, F32)
            acc_ref[...] = jnp.zeros(acc_ref.shape, F32)

            def scores(j):
                s_ref[j % 2] = jnp.dot(k_ref[0, 0, j * KEY_TILE:(j + 1) * KEY_TILE, :], qt,
                                       preferred_element_type=F32)

            def softmax(j):
                s = s_ref[j % 2]
                m_old = m_ref[...]
                m_new = jnp.maximum(m_old, jnp.max(s, axis=0, keepdims=True))
                al_ref[j % 2] = jnp.exp2(m_old - m_new)
                m_ref[...] = m_new
                p_ref[j % 2] = jnp.exp2(s - m_new).astype(BF16)

            def accumulate(j):
                pv = jnp.dot(vt_ref[0, 0, j], p_ref[j % 2], preferred_element_type=F32)
                acc_ref[...] = al_ref[j % 2] * acc_ref[...] + pv

            for t in range(n_kt + 2):
                if t < n_kt:
                    scores(t)
                if 1 <= t <= n_kt:
                    softmax(t - 1)
                if t >= 2:
                    accumulate(t - 2)
            acc = acc_ref[...]
            finish(sub, acc[0:HEAD_DIM], acc[HEAD_DIM:HEAD_DIM + 1])
            return carry

        jax.lax.fori_loop(0, n_sub, one_sub_tile, 0)

    jax.lax.cond(bounded_ref[0] != 0, bounded_scores, online_max)


def _score_bound(q_gain, k_gain):
    return (HEAD_DIM * Q_SCALE * BOUND_MARGIN) * jnp.max(jnp.abs(q_gain)) * jnp.max(jnp.abs(k_gain))


def _attention(qt, k, vt, bound, to_cast=()):
    B, _, n_qs, _, _ = qt.shape
    S = n_qs * Q_SUB
    n_kt = vt.shape[2]
    n_sub = max(1, min(UNROLLED_PASS_TILES // n_kt, n_qs))
    q_tile = n_sub * Q_SUB
    assert S % q_tile == 0
    lanes = Q_PER_KV * Q_SUB

    bounded = (bound <= MAX_SCORE_BOUND).astype(jnp.int32).reshape(1)

    n_i = S // q_tile
    n_steps = B * N_KV_HEADS * n_i
    slab_specs = []
    for w in to_cast:
        rows = w.shape[0] // n_steps
        assert rows * n_steps == w.shape[0] and rows % BF16_ROWS == 0
        slab_specs.append(pl.BlockSpec((rows, w.shape[1]),
                                       lambda b, g, i, *_: ((b * N_KV_HEADS + g) * n_i + i, 0)))

    grid_spec = pltpu.PrefetchScalarGridSpec(
        num_scalar_prefetch=1,
        grid=(B, N_KV_HEADS, S // q_tile),
        in_specs=[
            pl.BlockSpec((1, Q_PER_KV, n_sub, HEAD_DIM, Q_SUB), lambda b, g, i, *_: (b, g, i, 0, 0)),
            pl.BlockSpec((1, 1, S, HEAD_DIM), lambda b, g, i, *_: (b, g, 0, 0)),
            pl.BlockSpec((1, 1, n_kt, V_ROWS, KEY_TILE), lambda b, g, i, *_: (b, g, 0, 0, 0)),
        ] + slab_specs,
        out_specs=[pl.BlockSpec((1, q_tile, Q_PER_KV * HEAD_DIM), lambda b, g, i, *_: (b, i, g))]
        + slab_specs,
        scratch_shapes=[
            pltpu.VMEM((2, KEY_TILE, lanes), F32),
            pltpu.VMEM((2, KEY_TILE, lanes), BF16),
            pltpu.VMEM((1, lanes), F32),
            pltpu.VMEM((2, 1, lanes), F32),
            pltpu.VMEM((V_ROWS, lanes), F32),
        ],
    )
    return pl.pallas_call(
        _attn_kernel,
        grid_spec=grid_spec,
        out_shape=[jax.ShapeDtypeStruct((B, S, ATTN_WIDTH), BF16)]
        + [jax.ShapeDtypeStruct(w.shape, BF16) for w in to_cast],
        compiler_params=pltpu.CompilerParams(
            dimension_semantics=("parallel", "parallel", "arbitrary"),
            vmem_limit_bytes=VMEM_LIMIT_BYTES),
        name="attention",
    )(bounded, qt, k, vt, *to_cast)


def _mix_mlp_kernel(x_ref, a_ref, u_ref, up_ref, un_ref, wp_ref, ps_ref, wo_ref, g2_ref,
                    wu_ref, wd_ref, y_ref):
    i = pl.program_id(1)
    nt = pl.num_programs(1)
    T = x_ref.shape[1]
    S = T * nt
    n_ext = T + 2 * POOL_HALO

    u = u_ref[0]
    ext = jnp.concatenate([jnp.where(i > 0, up_ref[0], 0.0), u,
                           jnp.where(i < nt - 1, un_ref[0], 0.0)], axis=0)

    def rows_from(v, k):
        return pltpu.roll(v, (-k) % n_ext, 0) if k % n_ext else v

    row = jax.lax.broadcasted_iota(jnp.int32, (POOL_HALO, 1), 0)
    t_first = i * T + row
    t_last = i * T + (T - POOL_HALO) + row

    centred = []
    for g, win in enumerate(POOL_WINDOWS):
        sl = slice(g * POOL_GROUP_DIM, (g + 1) * POOL_GROUP_DIM)
        half = win // 2
        run, length = ext[:, sl], 1
        while length < win:
            run = run + rows_from(run, -length)
            length *= 2
        wsum = rows_from(run, half - 1)[POOL_HALO:POOL_HALO + T]

        def clipped(tok):
            return (jnp.minimum(tok + half, S) - jnp.maximum(tok - half, 0)).astype(F32)

        ug = u[:, sl]
        inner = slice(POOL_HALO, T - POOL_HALO)
        centred.append(jnp.concatenate([
            wsum[:POOL_HALO] / clipped(t_first) - ug[:POOL_HALO],
            wsum[inner] * (1.0 / win) - ug[inner],
            wsum[T - POOL_HALO:] / clipped(t_last) - ug[T - POOL_HALO:]], axis=0).astype(BF16))

    rows_per = T // MLP_ROW_BLOCKS
    residual, normed = [], []
    for r in range(MLP_ROW_BLOCKS):
        rows = slice(r * rows_per, (r + 1) * rows_per)
        mixed = [a_ref[0, rows, :]]
        for pair in range(N_POOL_GROUPS // POOL_PAIR):
            d = jnp.concatenate([c[rows] for c in centred[pair * POOL_PAIR:(pair + 1) * POOL_PAIR]],
                                axis=1)
            z = jnp.dot(d, wp_ref[pair], preferred_element_type=F32)
            z = z * ps_ref[:, pair * MXU_DIM:(pair + 1) * MXU_DIM]
            mixed.append(z.astype(BF16))
        mix = jnp.dot(jnp.concatenate(mixed, axis=1), wo_ref[...], preferred_element_type=F32)
        x1 = x_ref[0, rows, :] + mix
        ms = jnp.mean(x1 * x1, axis=-1, keepdims=True)
        residual.append(x1)
        normed.append((x1 * jax.lax.rsqrt(ms + EPS) * g2_ref[...]).astype(BF16))

    for r in range(MLP_ROW_BLOCKS):
        acc = residual[r]
        for c in range(D_FF // FF_CHUNK):
            sl = slice(c * FF_CHUNK, (c + 1) * FF_CHUNK)
            f = jnp.dot(normed[r], wu_ref[:, sl], preferred_element_type=F32)
            f = jnp.square(jnp.maximum(f, 0.0)).astype(BF16)
            acc = acc + jnp.dot(f, wd_ref[sl, :], preferred_element_type=F32)
        y_ref[0, r * rows_per:(r + 1) * rows_per, :] = acc


def _mix_mlp(x, a, u, w_pool_bd, pool_scale, w_out, g2, w_up, w_down):
    B, S, _ = x.shape
    T = SEQ_TILE
    nt = S // T
    hb = T // POOL_HALO
    n_hb = S // POOL_HALO
    const2 = lambda b, i: (0, 0)
    resident = functools.partial(pl.BlockSpec, pipeline_mode=pl.Buffered(1))
    return pl.pallas_call(
        _mix_mlp_kernel,
        grid=(B, nt),
        in_specs=[
            pl.BlockSpec((1, T, D_MODEL), lambda b, i: (b, i, 0)),
            pl.BlockSpec((1, T, ATTN_WIDTH), lambda b, i: (b, i, 0)),
            pl.BlockSpec((1, T, POOL_WIDTH), lambda b, i: (b, i, 0)),
            pl.BlockSpec((1, POOL_HALO, POOL_WIDTH),
                         lambda b, i: (b, jnp.maximum(i * hb - 1, 0), 0)),
            pl.BlockSpec((1, POOL_HALO, POOL_WIDTH),
                         lambda b, i: (b, jnp.minimum((i + 1) * hb, n_hb - 1), 0)),
            resident((N_POOL_GROUPS // POOL_PAIR, MXU_DIM, MXU_DIM), lambda b, i: (0, 0, 0)),
            pl.BlockSpec((1, POOL_WIDTH), const2),
            resident((D_MODEL, D_MODEL), const2),
            pl.BlockSpec((1, D_MODEL), const2),
            resident((D_MODEL, D_FF), const2),
            resident((D_FF, D_MODEL), const2),
        ],
        out_specs=pl.BlockSpec((1, T, D_MODEL), lambda b, i: (b, i, 0)),
        out_shape=jax.ShapeDtypeStruct((B, S, D_MODEL), F32),
        compiler_params=pltpu.CompilerParams(
            dimension_semantics=("parallel", "parallel"),
            vmem_limit_bytes=VMEM_LIMIT_BYTES),
        name="mix_mlp",
    )(x, a, u, u, u, w_pool_bd, pool_scale, w_out, g2, w_up, w_down)


def _layer(x, g1, w_u, w_tr, qg, kg, w_pool_bd, pool_scale, g2, mlp_weights, tables):
    to_cast = tuple(w for w in mlp_weights if w.dtype != BF16)
    qt, k, vt, u = _in_proj(x, g1, w_u, w_tr, qg, kg, tables)
    a, *cast = _attention(qt, k, vt, _score_bound(qg, kg), to_cast)
    cast = iter(cast)
    w_out, w_up, w_down = (w if w.dtype == BF16 else next(cast) for w in mlp_weights)
    y = _mix_mlp(x, a, u, w_pool_bd, pool_scale, w_out, g2, w_up, w_down)
    return y, (w_out, w_up, w_down)


def _trunk(x, layers):
    tables = _rope_tables(x.shape[1])
    rounded = []
    for *params, mlp_weights in layers:
        x, mlp_weights = _layer(x, *params, mlp_weights, tables)
        rounded.append((*params, mlp_weights))
    return x, rounded


def _pool_block_diagonal(w_pool):
    c = POOL_GROUP_DIM
    out = jnp.zeros((N_POOL_GROUPS // POOL_PAIR, MXU_DIM, MXU_DIM), w_pool.dtype)
    for g in range(N_POOL_GROUPS):
        pair, slot = divmod(g, POOL_PAIR)
        out = out.at[pair, slot * c:(slot + 1) * c, slot * c:(slot + 1) * c].set(w_pool[g])
    return out


def kernel(x_prompt, x_sample, norm1_g, w_in, q_norm_g, k_norm_g, w_pool, pool_scale, w_out,
           norm2_g, w_up, w_down):
    depth = w_in.shape[0]
    qkv_end = ATTN_WIDTH + 2 * KV_WIDTH
    layers = []
    for l in range(depth):
        w = w_in[l].astype(BF16)
        layers.append((
            norm1_g[l][None, :],
            w[:, qkv_end:],
            w[:, :qkv_end].T,
            q_norm_g[l][:, None],
            k_norm_g[l][:, None],
            _pool_block_diagonal(w_pool[l].astype(BF16)),
            pool_scale[l][None, :],
            norm2_g[l][None, :],
            (w_out[l], w_up[l], w_down[l]),
        ))
    y_prompt, layers = _trunk(x_prompt, layers)
    y_sample, _ = _trunk(x_sample, layers)
    return (y_prompt, y_sample)
```

```python
import functools
import math

import jax
import jax.numpy as jnp
from jax.experimental import pallas as pl
from jax.experimental.pallas import tpu as pltpu

D_MODEL = 1024
GRID_W = 64
HEAD_DIM = 64
N_Q_HEADS = 8
N_KV_HEADS = 2
Q_PER_KV = N_Q_HEADS // N_KV_HEADS
ATTN_WIDTH = N_Q_HEADS * HEAD_DIM
KV_WIDTH = N_KV_HEADS * HEAD_DIM
POOL_WINDOWS = (2, 4, 8, 16)
N_POOL_GROUPS = len(POOL_WINDOWS)
POOL_WIDTH = D_MODEL - ATTN_WIDTH
POOL_GROUP_DIM = POOL_WIDTH // N_POOL_GROUPS
D_FF = 4 * D_MODEL
ROPE_THETA = 10000.0
EPS = 1e-6
N_FREQ = HEAD_DIM // 4

LANES = 128
SUBLANES = 8
BF16_ROWS = 16
MXU_DIM = 256
MXU_ROWS = 256
POOL_HALO = 8
assert all(w & (w - 1) == 0 and w // 2 <= POOL_HALO for w in POOL_WINDOWS)
POOL_PAIR = MXU_DIM // POOL_GROUP_DIM
assert N_POOL_GROUPS % POOL_PAIR == 0
VMEM_LIMIT_BYTES = 56 * 1024 * 1024

PROJ_TILE = 1024
SEQ_TILE = 1024
KEY_TILE = 512
Q_SUB = 128
UNROLLED_PASS_TILES = 64
FF_CHUNK = 1024
MLP_ROW_BLOCKS = 4
V_ROWS = HEAD_DIM + BF16_ROWS
MAX_SCORE_BOUND = 30.0
BOUND_MARGIN = 1.02
Q_SCALE = HEAD_DIM ** -0.5 * math.log2(math.e)
STREAM_KEYS = 256
STREAM_AHEAD = 2

F32 = jnp.float32
BF16 = jnp.bfloat16

TR_WIDTH = ATTN_WIDTH + 2 * KV_WIDTH


def _rope_tables(seq_len):
    rows = seq_len // GRID_W
    row_ids = jnp.broadcast_to(jnp.arange(rows, dtype=F32)[:, None], (rows, GRID_W)).reshape(-1)
    col_ids = jnp.broadcast_to(jnp.arange(GRID_W, dtype=F32)[None, :], (rows, GRID_W)).reshape(-1)
    inv_freq = ROPE_THETA ** (-jnp.arange(N_FREQ, dtype=F32) / N_FREQ)
    ang = jnp.stack([row_ids[:, None] * inv_freq, col_ids[:, None] * inv_freq], axis=1)
    cos, sin = jnp.cos(ang), jnp.sin(ang)
    cos_h = jnp.stack([cos, cos], axis=2).reshape(seq_len, HEAD_DIM)
    sin_sg = jnp.stack([-sin, sin], axis=2).reshape(seq_len, HEAD_DIM)
    return cos_h.T, sin_sg.T


def _in_proj_kernel(x_ref, g1_ref, w_u_ref, w_tr_ref, qg_ref, kg_ref, cos_t_ref, sin_t_ref,
                    qt_ref, k_ref, vt_ref, u_ref):
    x = x_ref[0]
    ms = jnp.mean(x * x, axis=-1, keepdims=True)
    h = (x * jax.lax.rsqrt(ms + EPS) * g1_ref[...]).astype(BF16)
    n_tok = h.shape[0]
    for r in range(0, n_tok, MXU_ROWS):
        u_ref[0, r:r + MXU_ROWS, :] = jnp.dot(h[r:r + MXU_ROWS], w_u_ref[...],
                                              preferred_element_type=F32)
    tr = jnp.concatenate(
        [jax.lax.dot_general(w_tr_ref[r:r + MXU_ROWS, :], h, (((1,), (1,)), ((), ())),
                             preferred_element_type=F32)
         for r in range(0, TR_WIDTH, MXU_ROWS)], axis=0)
    cos_t, sin_t = cos_t_ref[...], sin_t_ref[...]

    def norm_rope(t, gain):
        tn = t * jax.lax.rsqrt(jnp.mean(t * t, axis=0, keepdims=True) + EPS) * gain
        partner = jnp.concatenate([tn[N_FREQ:2 * N_FREQ], tn[0:N_FREQ],
                                   tn[3 * N_FREQ:4 * N_FREQ], tn[2 * N_FREQ:3 * N_FREQ]], axis=0)
        return tn * cos_t + partner * sin_t

    q_gain = qg_ref[...] * Q_SCALE
    for hd in range(N_Q_HEADS):
        qr = norm_rope(tr[hd * HEAD_DIM:(hd + 1) * HEAD_DIM, :], q_gain).astype(BF16)
        for c in range(n_tok // Q_SUB):
            qt_ref[0, hd, c] = qr[:, c * Q_SUB:(c + 1) * Q_SUB]

    k_gain = kg_ref[...]
    kr_heads = [norm_rope(tr[ATTN_WIDTH + g * HEAD_DIM:ATTN_WIDTH + (g + 1) * HEAD_DIM, :], k_gain)
                for g in range(N_KV_HEADS)]
    k_tok = jnp.concatenate(kr_heads, axis=0).T
    for g in range(N_KV_HEADS):
        k_ref[0, g] = k_tok[:, g * HEAD_DIM:(g + 1) * HEAD_DIM].astype(BF16)

    row = jax.lax.broadcasted_iota(jnp.int32, (BF16_ROWS, KEY_TILE), 0)
    ones_rows = jnp.where(row == 0, 1.0, 0.0).astype(BF16)
    for g in range(N_KV_HEADS):
        lo = ATTN_WIDTH + KV_WIDTH + g * HEAD_DIM
        vg = tr[lo:lo + HEAD_DIM, :].astype(BF16)
        for c in range(n_tok // KEY_TILE):
            vt_ref[0, g, c, 0:HEAD_DIM, :] = vg[:, c * KEY_TILE:(c + 1) * KEY_TILE]
            vt_ref[0, g, c, HEAD_DIM:, :] = ones_rows


def _in_proj(x, g1, w_u, w_tr, qg, kg, tables):
    B, S, _ = x.shape
    T = PROJ_TILE
    assert S % T == 0 and T % KEY_TILE == 0 and T % Q_SUB == 0
    nt = S // T
    const = lambda b, i: (0, 0)
    cos_t, sin_t = tables
    return pl.pallas_call(
        _in_proj_kernel,
        grid=(B, nt),
        in_specs=[
            pl.BlockSpec((1, T, D_MODEL), lambda b, i: (b, i, 0)),
            pl.BlockSpec((1, D_MODEL), const),
            pl.BlockSpec((D_MODEL, POOL_WIDTH), const),
            pl.BlockSpec((TR_WIDTH, D_MODEL), const),
            pl.BlockSpec((HEAD_DIM, 1), const),
            pl.BlockSpec((HEAD_DIM, 1), const),
            pl.BlockSpec((HEAD_DIM, T), lambda b, i: (0, i)),
            pl.BlockSpec((HEAD_DIM, T), lambda b, i: (0, i)),
        ],
        out_specs=[
            pl.BlockSpec((1, N_Q_HEADS, T // Q_SUB, HEAD_DIM, Q_SUB), lambda b, i: (b, 0, i, 0, 0)),
            pl.BlockSpec((1, N_KV_HEADS, T, HEAD_DIM), lambda b, i: (b, 0, i, 0)),
            pl.BlockSpec((1, N_KV_HEADS, T // KEY_TILE, V_ROWS, KEY_TILE),
                         lambda b, i: (b, 0, i, 0, 0)),
            pl.BlockSpec((1, T, POOL_WIDTH), lambda b, i: (b, i, 0)),
        ],
        out_shape=[
            jax.ShapeDtypeStruct((B, N_Q_HEADS, S // Q_SUB, HEAD_DIM, Q_SUB), BF16),
            jax.ShapeDtypeStruct((B, N_KV_HEADS, S, HEAD_DIM), BF16),
            jax.ShapeDtypeStruct((B, N_KV_HEADS, S // KEY_TILE, V_ROWS, KEY_TILE), BF16),
            jax.ShapeDtypeStruct((B, S, POOL_WIDTH), F32),
        ],
        compiler_params=pltpu.CompilerParams(
            dimension_semantics=("parallel", "parallel"),
            vmem_limit_bytes=VMEM_LIMIT_BYTES),
        name="in_proj",
    )(x, g1, w_u, w_tr, qg, kg, cos_t, sin_t)


def _attn_kernel(bounded_ref, qt_ref, k_ref, vt_ref, *rest):
    n_cast = (len(rest) - 6) // 2
    o_ref = rest[n_cast]
    s_ref, p_ref, m_ref, al_ref, acc_ref = rest[2 * n_cast + 1:]
    for src_ref, dst_ref in zip(rest[:n_cast], rest[n_cast + 1:2 * n_cast + 1]):
        dst_ref[...] = src_ref[...].astype(BF16)

    n_kt = vt_ref.shape[2]
    n_sub = qt_ref.shape[2]
    lanes = Q_PER_KV * Q_SUB

    def queries(sub):
        return jnp.concatenate([qt_ref[0, h, sub] for h in range(Q_PER_KV)], axis=1)

    def finish(sub, acc, den):
        out_t = acc / den
        pairs = [jnp.concatenate([out_t[:, h * Q_SUB:(h + 1) * Q_SUB] for h in (p, p + 1)], axis=0).T
                 for p in range(0, Q_PER_KV, 2)]
        out = jnp.concatenate(pairs, axis=1)
        o_ref[0, pl.ds(pl.multiple_of(sub * Q_SUB, Q_SUB), Q_SUB), :] = out.astype(o_ref.dtype)

    def bounded_scores():
        per_tile = KEY_TILE // STREAM_KEYS
        n_blk = n_kt * per_tile
        qts = {}

        def scores(item):
            sub, b = item
            if sub not in qts:
                qts[sub] = queries(sub)
            return jnp.dot(k_ref[0, 0, b * STREAM_KEYS:(b + 1) * STREAM_KEYS, :], qts[sub],
                           preferred_element_type=F32)

        items = [(sub, b) for sub in range(n_sub) for b in range(n_blk)]
        pending = [scores(it) for it in items[:STREAM_AHEAD]]
        for n, (sub, b) in enumerate(items):
            if n + STREAM_AHEAD < len(items):
                pending.append(scores(items[n + STREAM_AHEAD]))
            if b == 0:
                acc = jnp.zeros((HEAD_DIM, lanes), F32)
                den = jnp.zeros((SUBLANES, lanes), F32)
            p = jnp.exp2(pending.pop(0))
            den = den + jnp.sum(p.reshape(STREAM_KEYS // SUBLANES, SUBLANES, lanes), axis=0)
            j, part = divmod(b, per_tile)
            vt = vt_ref[0, 0, j, 0:HEAD_DIM, part * STREAM_KEYS:(part + 1) * STREAM_KEYS]
            acc = acc + jnp.dot(vt, p.astype(BF16), preferred_element_type=F32)
            if b == n_blk - 1:
                finish(sub, acc, jnp.sum(den, axis=0, keepdims=True))

    def online_max():
        def one_sub_tile(sub, carry):
            qt = queries(sub)
            m_ref[...] = jnp.full(m_ref.shape, -1e30, F32)
            acc_ref[...] = jnp.zeros(acc_ref.shape, F32)

            def scores(j):
                s_ref[j % 2] = jnp.dot(k_ref[0, 0, j * KEY_TILE:(j + 1) * KEY_TILE, :], qt,
                                       preferred_element_type=F32)

            def softmax(j):
                s = s_ref[j % 2]
                m_old = m_ref[...]
                m_new = jnp.maximum(m_old, jnp.max(s, axis=0, keepdims=True))
                al_ref[j % 2] = jnp.exp2(m_old - m_new)
                m_ref[...] = m_new
                p_ref[j % 2] = jnp.exp2(s - m_new).astype(BF16)

            def accumulate(j):
                pv = jnp.dot(vt_ref[0, 0, j], p_ref[j % 2], preferred_element_type=F32)
                acc_ref[...] = al_ref[j % 2] * acc_ref[...] + pv

            for t in range(n_kt + 2):
                if t < n_kt:
                    scores(t)
                if 1 <= t <= n_kt:
                    softmax(t - 1)
                if t >= 2:
                    accumulate(t - 2)
            acc = acc_ref[...]
            finish(sub, acc[0:HEAD_DIM], acc[HEAD_DIM:HEAD_DIM + 1])
            return carry

        jax.lax.fori_loop(0, n_sub, one_sub_tile, 0)

    jax.lax.cond(bounded_ref[0] != 0, bounded_scores, online_max)


def _score_bound(q_gain, k_gain):
    return (HEAD_DIM * Q_SCALE * BOUND_MARGIN) * jnp.max(jnp.abs(q_gain)) * jnp.max(jnp.abs(k_gain))


def _attention(qt, k, vt, bound, to_cast=()):
    B, _, n_qs, _, _ = qt.shape
    S = n_qs * Q_SUB
    n_kt = vt.shape[2]
    n_sub = max(1, min(UNROLLED_PASS_TILES // n_kt, n_qs))
    q_tile = n_sub * Q_SUB
    assert S % q_tile == 0
    lanes = Q_PER_KV * Q_SUB

    bounded = (bound <= MAX_SCORE_BOUND).astype(jnp.int32).reshape(1)

    n_i = S // q_tile
    n_steps = B * N_KV_HEADS * n_i
    slab_specs = []
    for w in to_cast:
        rows = w.shape[0] // n_steps
        assert rows * n_steps == w.shape[0] and rows % BF16_ROWS == 0
        slab_specs.append(pl.BlockSpec((rows, w.shape[1]),
                                       lambda b, g, i, *_: ((b * N_KV_HEADS + g) * n_i + i, 0)))

    grid_spec = pltpu.PrefetchScalarGridSpec(
        num_scalar_prefetch=1,
        grid=(B, N_KV_HEADS, S // q_tile),
        in_specs=[
            pl.BlockSpec((1, Q_PER_KV, n_sub, HEAD_DIM, Q_SUB), lambda b, g, i, *_: (b, g, i, 0, 0)),
            pl.BlockSpec((1, 1, S, HEAD_DIM), lambda b, g, i, *_: (b, g, 0, 0)),
            pl.BlockSpec((1, 1, n_kt, V_ROWS, KEY_TILE), lambda b, g, i, *_: (b, g, 0, 0, 0)),
        ] + slab_specs,
        out_specs=[pl.BlockSpec((1, q_tile, Q_PER_KV * HEAD_DIM), lambda b, g, i, *_: (b, i, g))]
        + slab_specs,
        scratch_shapes=[
            pltpu.VMEM((2, KEY_TILE, lanes), F32),
            pltpu.VMEM((2, KEY_TILE, lanes), BF16),
            pltpu.VMEM((1, lanes), F32),
            pltpu.VMEM((2, 1, lanes), F32),
            pltpu.VMEM((V_ROWS, lanes), F32),
        ],
    )
    return pl.pallas_call(
        _attn_kernel,
        grid_spec=grid_spec,
        out_shape=[jax.ShapeDtypeStruct((B, S, ATTN_WIDTH), BF16)]
        + [jax.ShapeDtypeStruct(w.shape, BF16) for w in to_cast],
        compiler_params=pltpu.CompilerParams(
            dimension_semantics=("parallel", "parallel", "arbitrary"),
            vmem_limit_bytes=VMEM_LIMIT_BYTES),
        name="attention",
    )(bounded, qt, k, vt, *to_cast)


def _mix_mlp_kernel(x_ref, a_ref, u_ref, up_ref, un_ref, wp_ref, ps_ref, wo_ref, g2_ref,
                    wu_ref, wd_ref, y_ref):
    i = pl.program_id(1)
    nt = pl.num_programs(1)
    T = x_ref.shape[1]
    S = T * nt
    n_ext = T + 2 * POOL_HALO

    u = u_ref[0]
    ext = jnp.concatenate([jnp.where(i > 0, up_ref[0], 0.0), u,
                           jnp.where(i < nt - 1, un_ref[0], 0.0)], axis=0)

    def rows_from(v, k):
        return pltpu.roll(v, (-k) % n_ext, 0) if k % n_ext else v

    row = jax.lax.broadcasted_iota(jnp.int32, (POOL_HALO, 1), 0)
    t_first = i * T + row
    t_last = i * T + (T - POOL_HALO) + row

    centred = []
    for g, win in enumerate(POOL_WINDOWS):
        sl = slice(g * POOL_GROUP_DIM, (g + 1) * POOL_GROUP_DIM)
        half = win // 2
        run, length = ext[:, sl], 1
        while length < win:
            run = run + rows_from(run, -length)
            length *= 2
        wsum = rows_from(run, half - 1)[POOL_HALO:POOL_HALO + T]

        def clipped(tok):
            return (jnp.minimum(tok + half, S) - jnp.maximum(tok - half, 0)).astype(F32)

        ug = u[:, sl]
        inner = slice(POOL_HALO, T - POOL_HALO)
        centred.append(jnp.concatenate([
            wsum[:POOL_HALO] / clipped(t_first) - ug[:POOL_HALO],
            wsum[inner] * (1.0 / win) - ug[inner],
            wsum[T - POOL_HALO:] / clipped(t_last) - ug[T - POOL_HALO:]], axis=0).astype(BF16))

    rows_per = T // MLP_ROW_BLOCKS
    residual, normed = [], []
    for r in range(MLP_ROW_BLOCKS):
        rows = slice(r * rows_per, (r + 1) * rows_per)
        mixed = [a_ref[0, rows, :]]
        for pair in range(N_POOL_GROUPS // POOL_PAIR):
            d = jnp.concatenate([c[rows] for c in centred[pair * POOL_PAIR:(pair + 1) * POOL_PAIR]],
                                axis=1)
            z = jnp.dot(d, wp_ref[pair], preferred_element_type=F32)
            z = z * ps_ref[:, pair * MXU_DIM:(pair + 1) * MXU_DIM]
            mixed.append(z.astype(BF16))
        mix = jnp.dot(jnp.concatenate(mixed, axis=1), wo_ref[...], preferred_element_type=F32)
        x1 = x_ref[0, rows, :] + mix
        ms = jnp.mean(x1 * x1, axis=-1, keepdims=True)
        residual.append(x1)
        normed.append((x1 * jax.lax.rsqrt(ms + EPS) * g2_ref[...]).astype(BF16))

    for r in range(MLP_ROW_BLOCKS):
        acc = residual[r]
        for c in range(D_FF // FF_CHUNK):
            sl = slice(c * FF_CHUNK, (c + 1) * FF_CHUNK)
            f = jnp.dot(normed[r], wu_ref[:, sl], preferred_element_type=F32)
            f = jnp.square(jnp.maximum(f, 0.0)).astype(BF16)
            acc = acc + jnp.dot(f, wd_ref[sl, :], preferred_element_type=F32)
        y_ref[0, r * rows_per:(r + 1) * rows_per, :] = acc


def _mix_mlp(x, a, u, w_pool_bd, pool_scale, w_out, g2, w_up, w_down):
    B, S, _ = x.shape
    T = SEQ_TILE
    nt = S // T
    hb = T // POOL_HALO
    n_hb = S // POOL_HALO
    const2 = lambda b, i: (0, 0)
    resident = functools.partial(pl.BlockSpec, pipeline_mode=pl.Buffered(1))
    return pl.pallas_call(
        _mix_mlp_kernel,
        grid=(B, nt),
        in_specs=[
            pl.BlockSpec((1, T, D_MODEL), lambda b, i: (b, i, 0)),
            pl.BlockSpec((1, T, ATTN_WIDTH), lambda b, i: (b, i, 0)),
            pl.BlockSpec((1, T, POOL_WIDTH), lambda b, i: (b, i, 0)),
            pl.BlockSpec((1, POOL_HALO, POOL_WIDTH),
                         lambda b, i: (b, jnp.maximum(i * hb - 1, 0), 0)),
            pl.BlockSpec((1, POOL_HALO, POOL_WIDTH),
                         lambda b, i: (b, jnp.minimum((i + 1) * hb, n_hb - 1), 0)),
            resident((N_POOL_GROUPS // POOL_PAIR, MXU_DIM, MXU_DIM), lambda b, i: (0, 0, 0)),
            pl.BlockSpec((1, POOL_WIDTH), const2),
            resident((D_MODEL, D_MODEL), const2),
            pl.BlockSpec((1, D_MODEL), const2),
            resident((D_MODEL, D_FF), const2),
            resident((D_FF, D_MODEL), const2),
        ],
        out_specs=pl.BlockSpec((1, T, D_MODEL), lambda b, i: (b, i, 0)),
        out_shape=jax.ShapeDtypeStruct((B, S, D_MODEL), F32),
        compiler_params=pltpu.CompilerParams(
            dimension_semantics=("parallel", "parallel"),
            vmem_limit_bytes=VMEM_LIMIT_BYTES),
        name="mix_mlp",
    )(x, a, u, u, u, w_pool_bd, pool_scale, w_out, g2, w_up, w_down)


def _layer(x, g1, w_u, w_tr, qg, kg, w_pool_bd, pool_scale, g2, mlp_weights, tables):
    to_cast = tuple(w for w in mlp_weights if w.dtype != BF16)
    qt, k, vt, u = _in_proj(x, g1, w_u, w_tr, qg, kg, tables)
    a, *cast = _attention(qt, k, vt, _score_bound(qg, kg), to_cast)
    cast = iter(cast)
    w_out, w_up, w_down = (w if w.dtype == BF16 else next(cast) for w in mlp_weights)
    y = _mix_mlp(x, a, u, w_pool_bd, pool_scale, w_out, g2, w_up, w_down)
    return y, (w_out, w_up, w_down)


def _trunk(x, layers, tables):
    rounded = []
    for *params, mlp_weights in layers:
        x, mlp_weights = _layer(x, *params, mlp_weights, tables)
        rounded.append((*params, mlp_weights))
    return x, rounded


def _pool_block_diagonal(w_pool):
    c = POOL_GROUP_DIM
    out = jnp.zeros((N_POOL_GROUPS // POOL_PAIR, MXU_DIM, MXU_DIM), w_pool.dtype)
    for g in range(N_POOL_GROUPS):
        pair, slot = divmod(g, POOL_PAIR)
        out = out.at[pair, slot * c:(slot + 1) * c, slot * c:(slot + 1) * c].set(w_pool[g])
    return out


def kernel(x_prompt, x_sample, norm1_g, w_in, q_norm_g, k_norm_g, w_pool, pool_scale, w_out,
           norm2_g, w_up, w_down):
    depth = w_in.shape[0]
    qkv_end = ATTN_WIDTH + 2 * KV_WIDTH
    layers = []
    for l in range(depth):
        w = w_in[l].astype(BF16)
        layers.append((
            norm1_g[l][None, :],
            w[:, qkv_end:],
            w[:, :qkv_end].T,
            q_norm_g[l][:, None],
            k_norm_g[l][:, None],
            _pool_block_diagonal(w_pool[l].astype(BF16)),
            pool_scale[l][None, :],
            norm2_g[l][None, :],
            (w_out[l], w_up[l], w_down[l]),
        ))
    tables = _rope_tables(max(x_prompt.shape[1], x_sample.shape[1]))
    y_prompt, layers = _trunk(x_prompt, layers, tables)
    y_sample, _ = _trunk(x_sample, layers, tables)
    return (y_prompt, y_sample)
```
